```python
import math, functools
import jax, jax.numpy as jnp
from jax import lax
import numpy as np

D_MODEL = 1024
BATCH = 2
SEQ = 8192
DEPTH = 4
DEC_BATCH = 128
DEC_SEQ = 8
PAST_LEN = 2048
PAGE_SIZE = 128

HEAD_DIM = 64
SB_HEADS = 12
DIFF_HEADS = 6
MEM_HEADS = 4
N_MEM = 256
MIX_WIDTH = SB_HEADS * HEAD_DIM
KV_WIDTH = MIX_WIDTH
MEM_WIDTH = MEM_HEADS * HEAD_DIM
IN_WIDTH = 3 * MIX_WIDTH + MEM_WIDTH
OUT_WIDTH = MIX_WIDTH + MEM_WIDTH
ROT_DIM = HEAD_DIM // 4
ROPE_THETA = 500000.0
QBLOCK = 128
N_GROUPS = 8
EXPERTS_PER_GROUP = 8
N_EXPERTS = N_GROUPS * EXPERTS_PER_GROUP
TOP_K_IN_GROUP = 2
EXPERT_FF = 256
MOE_BLOCK = 128
N_DIFF_LAYERS = DEPTH // 2
RMS_EPS = 1e-6

kernel_name = "stickbreak_diffattn_hmoe_decode_step"


def rmsnorm(x, g):
    xf = x.astype(jnp.float32)
    y = xf * lax.rsqrt(jnp.mean(xf * xf, axis=-1, keepdims=True) + RMS_EPS)
    return (y * g.astype(jnp.float32)).astype(x.dtype)


def partial_rotary(x, pos):
    half = ROT_DIM // 2
    inv_freq = 1.0 / (ROPE_THETA ** (jnp.arange(half, dtype=jnp.float32) * 2.0 / ROT_DIM))
    ang = pos.astype(jnp.float32)[:, None] * inv_freq[None, :]
    bshape = (pos.shape[0],) + (1,) * (x.ndim - 3) + (half,)
    cos = jnp.cos(ang).reshape(bshape)
    sin = jnp.sin(ang).reshape(bshape)
    x1 = x[..., :half]
    x2 = x[..., half:ROT_DIM]
    rot = jnp.concatenate([x1 * cos - x2 * sin, x2 * cos + x1 * sin], axis=-1).astype(x.dtype)
    return jnp.concatenate([rot, x[..., ROT_DIM:]], axis=-1)


def stick_breaking_block(q, qpos, k, kpos, v):
    z = jnp.einsum('bthd,bshd->bhts', q, k).astype(jnp.float32) * (HEAD_DIM ** -0.5)
    mask = kpos[None, :] < qpos[:, None]
    log_fail = jnp.where(mask, jax.nn.log_sigmoid(-z), 0.0)
    log_between = lax.cumsum(log_fail, axis=3, reverse=True) - log_fail
    a = jnp.where(mask, jnp.exp(jax.nn.log_sigmoid(z) + log_between), 0.0)
    return jnp.einsum('bhts,bshd->bthd', a.astype(v.dtype), v)


def differential_block(q, qpos, k, kpos, v, lam, gain, lam_init):
    s = jnp.einsum('bthcd,bshcd->bhcts', q, k).astype(jnp.float32) * (HEAD_DIM ** -0.5)
    mask = kpos[None, :] <= qpos[:, None]
    p = jax.nn.softmax(jnp.where(mask, s, -jnp.inf), axis=-1)
    a = p[:, :, 0] - lam * p[:, :, 1]
    o = jnp.einsum('bhts,bshe->bthe', a.astype(v.dtype), v)
    return rmsnorm(o, gain) * (1.0 - lam_init)


def sweep_query_blocks(fn, q, qpos, k, kpos, v):
    B, T = q.shape[0], q.shape[1]
    if T % QBLOCK == 0 and T > QBLOCK:
        nb = T // QBLOCK
        qb = jnp.moveaxis(q.reshape((B, nb, QBLOCK) + q.shape[2:]), 1, 0)
        pb = qpos.reshape(nb, QBLOCK)
        ob = lax.map(lambda a: fn(a[0], a[1], k, kpos, v), (qb, pb))
        return jnp.moveaxis(ob, 0, 1).reshape((B, T) + ob.shape[3:])
    return fn(q, qpos, k, kpos, v)


def memory_kv(mem, g, w):
    kv = rmsnorm(mem, g) @ w
    return kv[..., :MEM_WIDTH], kv[..., MEM_WIDTH:]


def memory_attention(qm, mem_k, mem_v):
    B, T, _ = qm.shape
    q = qm.reshape(B, T, MEM_HEADS, HEAD_DIM)
    k = mem_k.reshape(B, -1, MEM_HEADS, HEAD_DIM)
    v = mem_v.reshape(B, -1, MEM_HEADS, HEAD_DIM)
    s = jnp.einsum('bthd,bmhd->bhtm', q, k).astype(jnp.float32) * (HEAD_DIM ** -0.5)
    p = jax.nn.softmax(s, axis=-1)
    o = jnp.einsum('bhtm,bmhd->bthd', p.astype(v.dtype), v)
    return o.reshape(B, T, MEM_WIDTH)


def attention_sublayer(layer, x, qpos, past_k, past_v, mem_k, mem_v, g_attn, w_in_l, w_out_l, lam_vecs, subln_gain):
    B, T, _ = x.shape
    proj = rmsnorm(x, g_attn) @ w_in_l
    q, k, v, qm = jnp.split(proj, [MIX_WIDTH, 2 * MIX_WIDTH, 3 * MIX_WIDTH], axis=-1)
    if layer % 2 == 0:
        qh = q.reshape(B, T, SB_HEADS, HEAD_DIM)
        k_new = k
        k_shape, v_shape = (SB_HEADS, HEAD_DIM), (SB_HEADS, HEAD_DIM)
        fn = stick_breaking_block
    else:
        qh = partial_rotary(q.reshape(B, T, DIFF_HEADS, 2, HEAD_DIM), qpos)
        k_new = partial_rotary(k.reshape(B, T, DIFF_HEADS, 2, HEAD_DIM), qpos).reshape(B, T, KV_WIDTH)
        k_shape, v_shape = (DIFF_HEADS, 2, HEAD_DIM), (DIFF_HEADS, 2 * HEAD_DIM)
        lam_init = 0.8 - 0.6 * math.exp(-0.3 * layer)
        lv = lam_vecs.astype(jnp.float32)
        lam = jnp.exp(jnp.sum(lv[0] * lv[1])) - jnp.exp(jnp.sum(lv[2] * lv[3])) + lam_init
        fn = functools.partial(differential_block, lam=lam, gain=subln_gain, lam_init=lam_init)
    if past_k is None:
        k_all, v_all = k_new, v
    else:
        k_all = jnp.concatenate([past_k.astype(k_new.dtype), k_new], axis=1)
        v_all = jnp.concatenate([past_v.astype(v.dtype), v], axis=1)
    S = k_all.shape[1]
    kpos = jnp.arange(S, dtype=jnp.int32)
    o = sweep_query_blocks(fn, qh, qpos, k_all.reshape((B, S) + k_shape), kpos,
                           v_all.reshape((B, S) + v_shape)).reshape(B, T, MIX_WIDTH)
    om = memory_attention(qm, mem_k, mem_v)
    y = x + jnp.concatenate([o, om], axis=-1) @ w_out_l
    return y, k_new, v


def routed_experts(x, expert, gate, w_gu, w_dn):
    N, D = x.shape
    K = expert.shape[1]
    NK = N * K
    e = expert.reshape(-1)
    tok = jnp.repeat(jnp.arange(N, dtype=jnp.int32), K)
    g = gate.reshape(-1)
    order = jnp.argsort(e)
    e_s, tok_s, g_s = e[order], tok[order], g[order]
    counts = jnp.bincount(e, length=N_EXPERTS)
    padded = (counts + MOE_BLOCK - 1) // MOE_BLOCK * MOE_BLOCK
    start = jnp.cumsum(counts) - counts
    pad_end = jnp.cumsum(padded)
    pad_start = pad_end - padded
    dest = pad_start[e_s] + jnp.arange(NK, dtype=jnp.int32) - start[e_s]
    n_blocks = -(-(NK + N_EXPERTS * (MOE_BLOCK - 1)) // MOE_BLOCK)
    L = n_blocks * MOE_BLOCK
    row_tok = jnp.full((L,), N, jnp.int32).at[dest].set(tok_s)
    row_gate = jnp.zeros((L,), x.dtype).at[dest].set(g_s.astype(x.dtype))
    block_expert = jnp.minimum(
        jnp.searchsorted(pad_end, jnp.arange(n_blocks, dtype=jnp.int32) * MOE_BLOCK, side='right'),
        N_EXPERTS - 1)
    x_pad = jnp.concatenate([x, jnp.zeros((1, D), x.dtype)], axis=0)
    xb = x_pad[row_tok].reshape(n_blocks, MOE_BLOCK, D)

    def expert_block(args):
        xblk, e_id = args
        a, u = jnp.split(xblk @ w_gu[e_id], 2, axis=-1)
        return (jax.nn.silu(a) * u) @ w_dn[e_id]

    yb = lax.map(expert_block, (xb, block_expert))
    y_rows = yb.reshape(L, D) * row_gate[:, None]
    return jax.ops.segment_sum(y_rows, row_tok, num_segments=N + 1)[:N]


def moe_sublayer(h, g_ffn, w_rg, b_rg, w_re, b_re, w_gu, w_dn):
    B, T, D = h.shape
    x = rmsnorm(h, g_ffn).reshape(B * T, D)
    lg = (x @ w_rg).astype(jnp.float32) + b_rg.astype(jnp.float32)
    pg = jax.nn.softmax(lg, axis=-1)
    g_idx = jnp.argmax(lg, axis=-1)
    g_w = jnp.take_along_axis(pg, g_idx[:, None], axis=1)
    le = (x @ w_re).astype(jnp.float32).reshape(-1, N_GROUPS, EXPERTS_PER_GROUP) + b_re.astype(jnp.float32)
    le = jnp.take_along_axis(le, g_idx[:, None, None], axis=1)[:, 0]
    top_v, top_i = lax.top_k(le, TOP_K_IN_GROUP)
    gate = g_w * jax.nn.softmax(top_v, axis=-1)
    expert = g_idx[:, None].astype(jnp.int32) * EXPERTS_PER_GROUP + top_i.astype(jnp.int32)
    y = routed_experts(x, expert, gate, w_gu, w_dn)
    return h + y.reshape(B, T, D)


def setup_inputs(seed: int = 0) -> dict:
    key = jax.random.key(seed)
    ks = jax.random.split(key, 24)
    f32 = jnp.float32
    n_pages = PAST_LEN // PAGE_SIZE
    n_phys = (DEC_BATCH * n_pages * 5) // 4

    def nrm(k, shape, scale):
        return jax.random.normal(k, shape, f32) * scale

    page_table = jax.random.permutation(ks[7], n_phys)[:DEC_BATCH * n_pages]
    page_table = page_table.reshape(DEC_BATCH, n_pages).astype(jnp.int32)
    return {
        'x_prompt': nrm(ks[0], (BATCH, SEQ, D_MODEL), 1.0),
        'x_sample': nrm(ks[1], (DEC_BATCH, DEC_SEQ, D_MODEL), 1.0),
        'mem_prompt': nrm(ks[2], (BATCH, N_MEM, D_MODEL), 1.0),
        'cache_k': nrm(ks[3], (DEPTH, n_phys, PAGE_SIZE, KV_WIDTH), 1.0),
        'cache_v': nrm(ks[4], (DEPTH, n_phys, PAGE_SIZE, KV_WIDTH), 1.0),
        'cache_mem_k': nrm(ks[5], (DEPTH, DEC_BATCH, N_MEM, MEM_WIDTH), 1.0),
        'cache_mem_v': nrm(ks[6], (DEPTH, DEC_BATCH, N_MEM, MEM_WIDTH), 1.0),
        'page_table': page_table,
        'norm_attn': 1.0 + nrm(ks[8], (DEPTH, D_MODEL), 0.05),
        'norm_mem': 1.0 + nrm(ks[9], (DEPTH, D_MODEL), 0.05),
        'norm_ffn': 1.0 + nrm(ks[10], (DEPTH, D_MODEL), 0.05),
        'norm_final': 1.0 + nrm(ks[11], (D_MODEL,), 0.05),
        'w_in': nrm(ks[12], (DEPTH, D_MODEL, IN_WIDTH), D_MODEL ** -0.5),
        'w_mem_kv': nrm(ks[13], (DEPTH, D_MODEL, 2 * MEM_WIDTH), D_MODEL ** -0.5),
        'w_out': nrm(ks[14], (DEPTH, OUT_WIDTH, D_MODEL), OUT_WIDTH ** -0.5),
        'diff_lambda': nrm(ks[15], (N_DIFF_LAYERS, 4, HEAD_DIM), 0.1),
        'diff_subln': 1.0 + nrm(ks[16], (N_DIFF_LAYERS, 2 * HEAD_DIM), 0.05),
        'w_router_group': nrm(ks[17], (DEPTH, D_MODEL, N_GROUPS), D_MODEL ** -0.5),
        'b_router_group': nrm(ks[18], (DEPTH, N_GROUPS), 0.01),
        'w_router_expert': nrm(ks[19], (DEPTH, D_MODEL, N_EXPERTS), D_MODEL ** -0.5),
        'b_router_expert': nrm(ks[20], (DEPTH, N_GROUPS, EXPERTS_PER_GROUP), 0.01),
        'w_expert_gu': nrm(ks[21], (DEPTH, N_EXPERTS, D_MODEL, 2 * EXPERT_FF), D_MODEL ** -0.5),
        'w_expert_down': nrm(ks[22], (DEPTH, N_EXPERTS, EXPERT_FF, D_MODEL), EXPERT_FF ** -0.5),
    }


def reference(x_prompt, x_sample, mem_prompt, cache_k, cache_v, cache_mem_k, cache_mem_v, page_table,
              norm_attn, norm_mem, norm_ffn, norm_final, w_in, w_mem_kv, w_out, diff_lambda, diff_subln,
              w_router_group, b_router_group, w_router_expert, b_router_expert, w_expert_gu, w_expert_down):
    past_len = page_table.shape[1] * PAGE_SIZE
    DB = x_sample.shape[0]
    pos_p = jnp.arange(x_prompt.shape[1], dtype=jnp.int32)
    pos_s = past_len + jnp.arange(x_sample.shape[1], dtype=jnp.int32)
    xp, xs = x_prompt, x_sample
    kp_l, vp_l, mkp_l, mvp_l, ks_l, vs_l = [], [], [], [], [], []
    for i in range(DEPTH):
        lam_vecs = diff_lambda[i // 2] if i % 2 == 1 else None
        subln = diff_subln[i // 2] if i % 2 == 1 else None
        mk_p, mv_p = memory_kv(mem_prompt, norm_mem[i], w_mem_kv[i])
        xp, kp, vp = attention_sublayer(i, xp, pos_p, None, None, mk_p, mv_p, norm_attn[i],
                                        w_in[i], w_out[i], lam_vecs, subln)
        xp = moe_sublayer(xp, norm_ffn[i], w_router_group[i], b_router_group[i], w_router_expert[i],
                          b_router_expert[i], w_expert_gu[i], w_expert_down[i])
        past_k = cache_k[i, page_table].reshape(DB, past_len, KV_WIDTH)
        past_v = cache_v[i, page_table].reshape(DB, past_len, KV_WIDTH)
        xs, ksn, vsn = attention_sublayer(i, xs, pos_s, past_k, past_v, cache_mem_k[i], cache_mem_v[i],
                                          norm_attn[i], w_in[i], w_out[i], lam_vecs, subln)
        xs = moe_sublayer(xs, norm_ffn[i], w_router_group[i], b_router_group[i], w_router_expert[i],
                          b_router_expert[i], w_expert_gu[i], w_expert_down[i])
        kp_l.append(kp)
        vp_l.append(vp)
        mkp_l.append(mk_p)
        mvp_l.append(mv_p)
        ks_l.append(ksn)
        vs_l.append(vsn)
    y_prompt = rmsnorm(xp, norm_final)
    y_sample = rmsnorm(xs, norm_final)
    k_prompt = jnp.stack(kp_l)
    v_prompt = jnp.stack(vp_l)
    mem_k_prompt = jnp.stack(mkp_l)
    mem_v_prompt = jnp.stack(mvp_l)
    k_sample = jnp.stack(ks_l)
    v_sample = jnp.stack(vs_l)
    return (y_prompt, y_sample, k_prompt, v_prompt, mem_k_prompt, mem_v_prompt, k_sample, v_sample)
```

```python
import functools
import math

import jax
import jax.numpy as jnp
from jax import lax
from jax.experimental import pallas as pl
from jax.experimental.pallas import tpu as pltpu

F32 = jnp.float32
BF16 = jnp.bfloat16

HEAD_DIM = 64
SB_HEADS = 12
DIFF_HEADS = 6
MEM_HEADS = 4
MIX_WIDTH = SB_HEADS * HEAD_DIM
MEM_WIDTH = MEM_HEADS * HEAD_DIM
ROT_DIM = HEAD_DIM // 4
ROPE_THETA = 500000.0
N_GROUPS = 8
EXPERTS_PER_GROUP = 8
N_EXPERTS = N_GROUPS * EXPERTS_PER_GROUP
EXPERT_FF = 256
RMS_EPS = 1e-6
QK_SCALE = HEAD_DIM ** -0.5

LANES = 128
ROW_TILE = 512
ATTN_TILE = 256
MOE_ROWS = 128
NEG_BIG = -1e30
SB_DEAD = -104.0
VMEM_LIMIT = 48 * 1024 * 1024
ROUTE_LANE0 = N_GROUPS


def _cparams(*sem):
    return pltpu.CompilerParams(dimension_semantics=sem, vmem_limit_bytes=VMEM_LIMIT)


def _rmsnorm_rows(x, g):
    ms = jnp.mean(x * x, axis=-1, keepdims=True)
    return x * lax.rsqrt(ms + RMS_EPS) * g


def _dot(a, b):
    return jnp.dot(a, b, preferred_element_type=F32)


def _dot_nt(a, b):
    return lax.dot_general(a, b, (((1,), (1,)), ((), ())), preferred_element_type=F32)


def _iota(shape, axis):
    return lax.broadcasted_iota(jnp.int32, shape, axis)


def _proj_body(rotary, x_ref, g_ref, w_ref, *rest):
    if rotary:
        cos_ref, sa_ref, sb_ref = rest[:3]
        rest = rest[3:]
    kf_ref, vf_ref, qb_ref, kb_ref, vb_ref, qm_ref = rest
    xn = _rmsnorm_rows(x_ref[...], g_ref[...]).astype(BF16)
    q = _dot(xn, w_ref[:, 0:MIX_WIDTH])
    k = _dot(xn, w_ref[:, MIX_WIDTH:2 * MIX_WIDTH])
    v = _dot(xn, w_ref[:, 2 * MIX_WIDTH:3 * MIX_WIDTH])
    qm = _dot(xn, w_ref[:, 3 * MIX_WIDTH:])
    if rotary:
        c, sa, sb = cos_ref[...], sa_ref[...], sb_ref[...]

        def rot(t):
            outs = []
            for j in range(MIX_WIDTH // LANES):
                s = t[:, j * LANES:(j + 1) * LANES]
                outs.append(s * c + pltpu.roll(s, LANES - ROT_DIM // 2, 1) * sa
                            + pltpu.roll(s, ROT_DIM // 2, 1) * sb)
            return jnp.concatenate(outs, axis=1)

        q = rot(q)
        k = rot(k)
    kf_ref[...] = k
    vf_ref[...] = v
    qb_ref[...] = (q * QK_SCALE).astype(BF16)
    kb_ref[...] = k.astype(BF16)
    vb_ref[...] = v.astype(BF16)
    qm_ref[...] = (qm * QK_SCALE).astype(BF16)


def _proj(x, g, w_bf16, rot_tables):
    n, d = x.shape
    tm = ROW_TILE
    rotary = rot_tables is not None
    row = lambda w: pl.BlockSpec((tm, w), lambda i: (i, 0))
    in_specs = [row(d), pl.BlockSpec((1, d), lambda i: (0, 0)),
                pl.BlockSpec(w_bf16.shape, lambda i: (0, 0))]
    args = [x, g.reshape(1, d), w_bf16]
    if rotary:
        in_specs += [row(LANES)] * 3
        args += list(rot_tables)
    out_shape = [jax.ShapeDtypeStruct((n, MIX_WIDTH), F32)] * 2 + \
                [jax.ShapeDtypeStruct((n, MIX_WIDTH), BF16)] * 3 + \
                [jax.ShapeDtypeStruct((n, MEM_WIDTH), BF16)]
    out_specs = [row(MIX_WIDTH)] * 5 + [row(MEM_WIDTH)]
    return pl.pallas_call(
        functools.partial(_proj_body, rotary),
        grid=(n // tm,), in_specs=in_specs, out_specs=out_specs, out_shape=out_shape,
        compiler_params=_cparams("parallel"), name="proj_rot" if rotary else "proj",
    )(*args)


def _rotary_tables(pos):
    half = ROT_DIM // 2
    inv_freq = 1.0 / (ROPE_THETA ** (jnp.arange(half, dtype=F32) * 2.0 / ROT_DIM))
    ang = pos.astype(F32)[:, None] * inv_freq[None, :]
    cos, sin = jnp.cos(ang), jnp.sin(ang)
    lane = jnp.arange(LANES) % HEAD_DIM
    f = lane % half
    c = jnp.where(lane[None, :] < ROT_DIM, cos[:, f], 1.0)
    sa = jnp.where(lane[None, :] < half, -sin[:, f], 0.0)
    sb = jnp.where((lane[None, :] >= half) & (lane[None, :] < ROT_DIM), sin[:, f], 0.0)
    return c, sa, sb


def _memkv_body(mem_ref, g_ref, w_ref, k_ref, v_ref):
    xn = _rmsnorm_rows(mem_ref[...], g_ref[...]).astype(BF16)
    kv = _dot(xn, w_ref[...].astype(BF16))
    k_ref[...] = kv[:, :MEM_WIDTH]
    v_ref[...] = kv[:, MEM_WIDTH:]


def _memkv(mem2d, norm_mem, w_mem_kv):
    depth, d = norm_mem.shape
    m = mem2d.shape[0]
    out = jax.ShapeDtypeStruct((depth, m, MEM_WIDTH), F32)
    ospec = pl.BlockSpec((None, m, MEM_WIDTH), lambda l: (l, 0, 0))
    return pl.pallas_call(
        _memkv_body, grid=(depth,),
        in_specs=[pl.BlockSpec((m, d), lambda l: (0, 0)),
                  pl.BlockSpec((None, 1, d), lambda l: (l, 0, 0)),
                  pl.BlockSpec((None, d, 2 * MEM_WIDTH), lambda l: (l, 0, 0))],
        out_specs=[ospec, ospec], out_shape=[out, out],
        compiler_params=_cparams("parallel"), name="mem_kv",
    )(mem2d, norm_mem.reshape(depth, 1, d), w_mem_kv)


def _neg_upper(tk):
    return jnp.where(_iota((tk, tk), 0) > _iota((tk, tk), 1), -1.0, 0.0).astype(BF16)


def _sb_chunk(qh, ks, vs, uneg, carry, acc, mask):
    z = _dot_nt(qh, ks)
    sp = jnp.maximum(z, 0.0) + jnp.log(1.0 + jnp.exp(-jnp.abs(z)))
    spm = sp if mask is None else jnp.where(mask, sp, 0.0)
    hi = spm.astype(BF16)
    lo = (spm - hi.astype(F32)).astype(BF16)
    between = _dot(hi, uneg) + _dot(lo, uneg) + carry
    a = jnp.exp(z - sp + between)
    if mask is not None:
        a = jnp.where(mask, a, 0.0)
    acc = acc + _dot(a.astype(BF16), vs)
    carry = carry - jnp.sum(spm, axis=-1, keepdims=True)
    return carry, acc


def _sm_chunk(qh, ks, vs, m, l, acc, mask):
    s = _dot_nt(qh, ks)
    if mask is not None:
        s = jnp.where(mask, s, NEG_BIG)
    m_new = jnp.maximum(m, jnp.max(s, axis=-1, keepdims=True))
    alpha = jnp.exp(m - m_new)
    p = jnp.exp(s - m_new)
    l = alpha * l + jnp.sum(p, axis=-1, keepdims=True)
    acc = alpha * acc + _dot(p.astype(BF16), vs)
    return m_new, l, acc


def _diff_lambda(lam_ref, lam_init):
    lv = lam_ref[...]
    s1 = jnp.sum(lv[0:1] * lv[1:2], axis=-1, keepdims=True)
    s2 = jnp.sum(lv[2:3] * lv[3:4], axis=-1, keepdims=True)
    return jnp.exp(s1) - jnp.exp(s2) + lam_init


def _head_rmsnorm(o, gain, lam_init):
    return _rmsnorm_rows(o, gain) * (1.0 - lam_init)


def _mem_attention(qbd, mk, mv):
    s = _dot_nt(qbd, mk)
    p = jnp.exp(s - jnp.max(s, axis=-1, keepdims=True))
    o = _dot(p.astype(BF16), mv)
    return o / jnp.sum(p, axis=-1, keepdims=True)


def _sb_prompt_body(q_ref, k_ref, v_ref, o_ref, acc_ref, carry_ref, *, t):
    i = pl.program_id(2)
    q = q_ref[...].astype(F32)
    lane = _iota((t, LANES), 1)
    qhs = (jnp.where(lane < HEAD_DIM, q, 0.0).astype(BF16), jnp.where(lane >= HEAD_DIM, q, 0.0).astype(BF16))
    uneg = _neg_upper(t)
    causal = _iota((t, t), 1) < _iota((t, t), 0)

    def process(c, mask):
        start = pl.multiple_of(c * t, t)
        ks = k_ref[pl.ds(start, t), :]
        vs = v_ref[pl.ds(start, t), :]
        for h in range(2):
            carry, acc = _sb_chunk(qhs[h], ks, vs, uneg, carry_ref[h], acc_ref[h], mask)
            carry_ref[h] = carry
            acc_ref[h] = acc

    def alive():
        return (jnp.max(carry_ref[...]) > SB_DEAD).astype(jnp.int32)

    carry_ref[...] = jnp.zeros_like(carry_ref)
    acc_ref[...] = jnp.zeros_like(acc_ref)
    process(i, causal)

    def body(st):
        process(st[0], None)
        return st[0] - 1, alive()

    lax.while_loop(lambda st: (st[0] >= 0) & (st[1] > 0), body, (i - 1, alive()))
    o_ref[...] = jnp.where(lane < HEAD_DIM, acc_ref[0], acc_ref[1]).astype(BF16)


def _diff_prompt_body(lam_ref, gain_ref, q_ref, k_ref, v_ref, o_ref, m_ref, l_ref, acc_ref, *, t, lam_init):
    i = pl.program_id(2)
    q = q_ref[...].astype(F32)
    lane = _iota((t, LANES), 1)
    qhs = (jnp.where(lane < HEAD_DIM, q, 0.0).astype(BF16), jnp.where(lane >= HEAD_DIM, q, 0.0).astype(BF16))
    causal = _iota((t, t), 1) <= _iota((t, t), 0)

    def process(c, mask):
        start = pl.multiple_of(c * t, t)
        ks = k_ref[pl.ds(start, t), :]
        vs = v_ref[pl.ds(start, t), :]
        for h in range(2):
            m, l, acc = _sm_chunk(qhs[h], ks, vs, m_ref[h], l_ref[h], acc_ref[h], mask)
            m_ref[h] = m
            l_ref[h] = l
            acc_ref[h] = acc

    m_ref[...] = jnp.full_like(m_ref, NEG_BIG)
    l_ref[...] = jnp.zeros_like(l_ref)
    acc_ref[...] = jnp.zeros_like(acc_ref)

    def body(c, carry):
        process(c, None)
        return carry

    lax.fori_loop(0, i, body, 0)
    process(i, causal)
    lam = _diff_lambda(lam_ref, lam_init)
    o = acc_ref[0] / l_ref[0] - lam * (acc_ref[1] / l_ref[1])
    o_ref[...] = _head_rmsnorm(o, gain_ref[...], lam_init).astype(BF16)


def _prompt_attention(qb, kb, vb, n_rows, batch, seq, diff_params):
    t = min(ATTN_TILE, seq)
    nq = seq // t
    nh = MIX_WIDTH // LANES
    qspec = pl.BlockSpec((t, LANES), lambda b, h, i: (b * nq + i, h))
    kvspec = pl.BlockSpec((seq, LANES), lambda b, h, i: (b, h))
    out_shape = jax.ShapeDtypeStruct((n_rows, MIX_WIDTH), BF16)
    if diff_params is None:
        body = functools.partial(_sb_prompt_body, t=t)
        in_specs, args = [qspec, kvspec, kvspec], [qb, kb, vb]
        scratch = [pltpu.VMEM((2, t, LANES), F32), pltpu.VMEM((2, t, 1), F32)]
        name = "sb_prompt"
    else:
        lam_vecs, gain, lam_init = diff_params
        body = functools.partial(_diff_prompt_body, t=t, lam_init=lam_init)
        in_specs = [pl.BlockSpec((4, HEAD_DIM), lambda b, h, i: (0, 0)),
                    pl.BlockSpec((1, LANES), lambda b, h, i: (0, 0)), qspec, kvspec, kvspec]
        args = [lam_vecs, gain.reshape(1, LANES), qb, kb, vb]
        scratch = [pltpu.VMEM((2, t, 1), F32), pltpu.VMEM((2, t, 1), F32), pltpu.VMEM((2, t, LANES), F32)]
        name = "diff_prompt"
    return pl.pallas_call(
        body, grid=(batch, nh, nq), in_specs=in_specs, out_specs=qspec, out_shape=out_shape,
        scratch_shapes=scratch, compiler_params=_cparams("parallel", "parallel", "arbitrary"), name=name,
    )(*args)


def _mem_prompt_body(q_ref, mk_ref, mv_ref, o_ref):
    q = q_ref[...].astype(F32)
    mk = mk_ref[...].astype(BF16)
    mv = mv_ref[...].astype(BF16)
    lane_head = _iota(q.shape, 1) // HEAD_DIM
    out = jnp.zeros(q.shape, F32)
    for h in range(MEM_HEADS):
        o = _mem_attention(jnp.where(lane_head == h, q, 0.0).astype(BF16), mk, mv)
        out = jnp.where(lane_head == h, o, out)
    o_ref[...] = out.astype(BF16)


def _mem_prompt(qm, mk, mv, n_rows, batch, seq):
    t = min(ROW_TILE, seq)
    nq = seq // t
    n_mem = mk.shape[1]
    qspec = pl.BlockSpec((t, MEM_WIDTH), lambda b, i: (b * nq + i, 0))
    mspec = pl.BlockSpec((None, n_mem, MEM_WIDTH), lambda b, i: (b, 0, 0))
    return pl.pallas_call(
        _mem_prompt_body, grid=(batch, nq), in_specs=[qspec, mspec, mspec], out_specs=qspec,
        out_shape=jax.ShapeDtypeStruct((n_rows, MEM_WIDTH), BF16),
        compiler_params=_cparams("parallel", "parallel"), name="mem_prompt",
    )(qm, mk, mv)


def _block_diag_queries(q, n_blocks, block_lanes):
    dt = q.shape[0]
    qt = jnp.concatenate([q] * n_blocks, axis=0)
    keep = (_iota(qt.shape, 0) // dt) == (_iota(qt.shape, 1) // block_lanes)
    return jnp.where(keep, qt, 0.0).astype(BF16)


def _fold_blocks(full, dt, block_lanes):
    keep = (_iota(full.shape, 0) // dt) == (_iota(full.shape, 1) // block_lanes)
    sel = jnp.where(keep, full, 0.0)
    out = sel[0:dt]
    for j in range(1, full.shape[0] // dt):
        out = out + sel[j * dt:(j + 1) * dt]
    return out


def _sample_body(pt_ref, *refs, n_pages, page, dt, diff, lam_init):
    del pt_ref
    if diff:
        lam_ref, gain_ref = refs[:2]
        refs = refs[2:]
    q_ref, kn_ref, vn_ref, qm_ref, mk_ref, mv_ref = refs[:6]
    k_pages = refs[6:6 + n_pages]
    v_pages = refs[6 + n_pages:6 + 2 * n_pages]
    o_ref, om_ref = refs[6 + 2 * n_pages:]

    n_maps = MIX_WIDTH // HEAD_DIM
    rows = n_maps * dt
    qbd = _block_diag_queries(q_ref[...].astype(F32), n_maps, HEAD_DIM)
    pad = jnp.zeros((page - dt, MIX_WIDTH), F32)
    kn = jnp.concatenate([kn_ref[...].astype(F32), pad], axis=0).astype(BF16)
    vn = jnp.concatenate([vn_ref[...].astype(F32), pad], axis=0).astype(BF16)
    tpos = _iota((rows, page), 0) % dt
    col = _iota((rows, page), 1)

    if not diff:
        uneg = _neg_upper(page)
        carry = jnp.zeros((rows, 1), F32)
        acc = jnp.zeros((rows, MIX_WIDTH), F32)
        carry, acc = _sb_chunk(qbd, kn, vn, uneg, carry, acc, col < tpos)
        for p in reversed(range(n_pages)):
            carry, acc = _sb_chunk(qbd, k_pages[p][...].astype(BF16), v_pages[p][...].astype(BF16),
                                   uneg, carry, acc, None)
        o = _fold_blocks(acc, dt, HEAD_DIM)
    else:
        m = jnp.full((rows, 1), NEG_BIG, F32)
        l = jnp.zeros((rows, 1), F32)
        acc = jnp.zeros((rows, MIX_WIDTH), F32)
        for p in range(n_pages):
            m, l, acc = _sm_chunk(qbd, k_pages[p][...].astype(BF16), v_pages[p][...].astype(BF16),
                                  m, l, acc, None)
        m, l, acc = _sm_chunk(qbd, kn, vn, m, l, acc, col <= tpos)
        lam = _diff_lambda(lam_ref, lam_init)
        second_map = ((_iota((rows, 1), 0) // dt) % 2) == 1
        weight = jnp.where(second_map, -lam, 1.0) / l
        o = _fold_blocks(acc * weight, 2 * dt, 2 * HEAD_DIM)
        o = o[0:dt] + o[dt:2 * dt]
        gain = gain_ref[...]
        o = jnp.concatenate(
            [_head_rmsnorm(o[:, j * LANES:(j + 1) * LANES], gain, lam_init) for j in range(DIFF_HEADS)], axis=1)
    o_ref[...] = o.astype(BF16)

    qmbd = _block_diag_queries(qm_ref[...].astype(F32), MEM_HEADS, HEAD_DIM)
    om = _mem_attention(qmbd, mk_ref[...].astype(BF16), mv_ref[...].astype(BF16))
    om_ref[...] = _fold_blocks(om, dt, HEAD_DIM).astype(BF16)


def _sample_attention(layer, q3, kn3, vn3, qm3, cache_k, cache_v, cache_mem_k, cache_mem_v, page_table,
                      diff_params):
    db, dt, _ = q3.shape
    n_pages = page_table.shape[1]
    page = cache_k.shape[2]
    n_mem = cache_mem_k.shape[2]
    diff = diff_params is not None

    def req(w):
        return pl.BlockSpec((None, dt, w), lambda b, pt: (b, 0, 0))

    def page_spec(p):
        return pl.BlockSpec((None, None, page, MIX_WIDTH), lambda b, pt: (layer, pt[b * n_pages + p], 0, 0))

    mem_spec = pl.BlockSpec((None, None, n_mem, MEM_WIDTH), lambda b, pt: (layer, b, 0, 0))
    in_specs = [req(MIX_WIDTH)] * 3 + [req(MEM_WIDTH), mem_spec, mem_spec]
    in_specs += [page_spec(p) for p in range(n_pages)] * 2
    args = [q3, kn3, vn3, qm3, cache_mem_k, cache_mem_v] + [cache_k] * n_pages + [cache_v] * n_pages
    lam_init = 0.0
    if diff:
        lam_vecs, gain, lam_init = diff_params
        in_specs = [pl.BlockSpec((4, HEAD_DIM), lambda b, pt: (0, 0)),
                    pl.BlockSpec((1, LANES), lambda b, pt: (0, 0))] + in_specs
        args = [lam_vecs, gain.reshape(1, LANES)] + args
    body = functools.partial(_sample_body, n_pages=n_pages, page=page, dt=dt, diff=diff, lam_init=lam_init)
    return pl.pallas_call(
        body,
        grid_spec=pltpu.PrefetchScalarGridSpec(
            num_scalar_prefetch=1, grid=(db,), in_specs=in_specs,
            out_specs=[req(MIX_WIDTH), req(MEM_WIDTH)]),
        out_shape=[jax.ShapeDtypeStruct((db, dt, MIX_WIDTH), BF16),
                   jax.ShapeDtypeStruct((db, dt, MEM_WIDTH), BF16)],
        compiler_params=_cparams("arbitrary"), name="diff_sample" if diff else "sb_sample",
    )(page_table.reshape(-1), *args)


def _out_router_body(x_ref, o_ref, om_ref, wo_ref, g_ref, wrh_ref, wrl_ref, br_ref,
                     y_ref, xn_ref, info_ref, cnt_ref, run_ref):
    @pl.when(pl.program_id(0) == 0)
    def _():
        run_ref[...] = jnp.zeros_like(run_ref)

    y = x_ref[...] + _dot(o_ref[...], wo_ref[0:MIX_WIDTH, :]) + _dot(om_ref[...], wo_ref[MIX_WIDTH:, :])
    y_ref[...] = y
    xn = _rmsnorm_rows(y, g_ref[...])
    xh = xn.astype(BF16)
    xn_ref[...] = xh
    xl = (xn - xh.astype(F32)).astype(BF16)
    wh, wl = wrh_ref[...], wrl_ref[...]
    logit = _dot(xh, wh) + _dot(xl, wh) + _dot(xh, wl) + br_ref[...]

    tm = logit.shape[0]
    lane = _iota((tm, LANES), 1)
    lg = jnp.where(lane < N_GROUPS, logit, -jnp.inf)
    mg = jnp.max(lg, axis=-1, keepdims=True)
    g_idx = jnp.min(jnp.where(lg == mg, lane, LANES), axis=-1, keepdims=True)
    g_w = 1.0 / jnp.sum(jnp.exp(lg - mg), axis=-1, keepdims=True)
    in_group = (lane >= ROUTE_LANE0) & (((lane - ROUTE_LANE0) >> 3) == g_idx) & (lane < ROUTE_LANE0 + N_EXPERTS)
    le = jnp.where(in_group, logit, -jnp.inf)
    v1 = jnp.max(le, axis=-1, keepdims=True)
    i1 = jnp.min(jnp.where(le == v1, lane, LANES), axis=-1, keepdims=True)
    le2 = jnp.where(lane == i1, -jnp.inf, le)
    v2 = jnp.max(le2, axis=-1, keepdims=True)
    i2 = jnp.min(jnp.where(le2 == v2, lane, LANES), axis=-1, keepdims=True)
    e21 = jnp.exp(v2 - v1)
    w1 = 1.0 / (1.0 + e21)
    w2 = e21 * w1

    hit1, hit2 = lane == i1, lane == i2
    onehot = jnp.where(hit1 | hit2, 1.0, 0.0)
    lower = jnp.where(_iota((tm, tm), 1) < _iota((tm, tm), 0), 1.0, 0.0).astype(BF16)
    before = _dot(lower, onehot.astype(BF16)) + run_ref[...]
    r1 = jnp.sum(jnp.where(hit1, before, 0.0), axis=-1, keepdims=True)
    r2 = jnp.sum(jnp.where(hit2, before, 0.0), axis=-1, keepdims=True)
    run_ref[...] = run_ref[...] + jnp.sum(onehot, axis=0, keepdims=True)
    cnt_ref[...] = run_ref[...]

    info = jnp.zeros((tm, LANES), F32)
    fields = ((i1 - ROUTE_LANE0).astype(F32), (i2 - ROUTE_LANE0).astype(F32), g_w * w1, g_w * w2, r1, r2)
    for j, val in enumerate(fields):
        info = jnp.where(lane == j, val, info)
    info_ref[...] = info


def _out_router(x, o, om, wo_bf16, g_ffn, wr_hi, wr_lo, b_route):
    n, d = x.shape
    tm = ROW_TILE
    row = lambda w: pl.BlockSpec((tm, w), lambda i: (i, 0))
    full = lambda a: pl.BlockSpec(a.shape, lambda i: (0, 0))
    g2 = g_ffn.reshape(1, d)
    return pl.pallas_call(
        _out_router_body, grid=(n // tm,),
        in_specs=[row(d), row(MIX_WIDTH), row(MEM_WIDTH), full(wo_bf16), full(g2), full(wr_hi), full(wr_lo),
                  full(b_route)],
        out_specs=[row(d), row(d), row(LANES), pl.BlockSpec((1, LANES), lambda i: (0, 0))],
        out_shape=[jax.ShapeDtypeStruct((n, d), F32), jax.ShapeDtypeStruct((n, d), BF16),
                   jax.ShapeDtypeStruct((n, LANES), F32), jax.ShapeDtypeStruct((1, LANES), F32)],
        scratch_shapes=[pltpu.VMEM((1, LANES), F32)],
        compiler_params=_cparams("arbitrary"), name="out_router",
    )(x, o, om, wo_bf16, g2, wr_hi, wr_lo, b_route)


def _expert_body(be_ref, nused_ref, x_ref, wgu_ref, wdn_ref, y_ref, wgu_s, wdn_s):
    i = pl.program_id(0)
    prev = be_ref[jnp.maximum(i - 1, 0)]

    @pl.when((i == 0) | (be_ref[i] != prev))
    def _():
        wgu_s[...] = wgu_ref[...].astype(BF16)
        wdn_s[...] = wdn_ref[...].astype(BF16)

    @pl.when(i < nused_ref[0])
    def _():
        h = _dot(x_ref[...], wgu_s[...])
        a, u = h[:, :EXPERT_FF], h[:, EXPERT_FF:]
        act = a * (1.0 / (1.0 + jnp.exp(-a))) * u
        y_ref[...] = _dot(act.astype(BF16), wdn_s[...]).astype(BF16)

    @pl.when(i >= nused_ref[0])
    def _():
        y_ref[...] = jnp.zeros_like(y_ref)


def _experts(layer, xs, block_expert, n_used, w_gu, w_dn):
    rows, d = xs.shape
    nb = rows // MOE_ROWS
    return pl.pallas_call(
        _expert_body,
        grid_spec=pltpu.PrefetchScalarGridSpec(
            num_scalar_prefetch=2, grid=(nb,),
            in_specs=[pl.BlockSpec((MOE_ROWS, d), lambda i, be, nu: (i, 0)),
                      pl.BlockSpec((None, None, d, 2 * EXPERT_FF), lambda i, be, nu: (layer, be[i], 0, 0)),
                      pl.BlockSpec((None, None, EXPERT_FF, d), lambda i, be, nu: (layer, be[i], 0, 0))],
            out_specs=pl.BlockSpec((MOE_ROWS, d), lambda i, be, nu: (i, 0)),
            scratch_shapes=[pltpu.VMEM((d, 2 * EXPERT_FF), BF16), pltpu.VMEM((EXPERT_FF, d), BF16)]),
        out_shape=jax.ShapeDtypeStruct((rows, d), BF16),
        compiler_params=_cparams("arbitrary"), name="experts",
    )(block_expert, n_used, xs, w_gu, w_dn)


def _moe(layer, y, xn, info, counts, w_gu, w_dn):
    n = y.shape[0]
    e = info[:, 0:2].astype(jnp.int32)
    gate = info[:, 2:4]
    rank = info[:, 4:6].astype(jnp.int32)
    cnt = counts[0, ROUTE_LANE0:ROUTE_LANE0 + N_EXPERTS].astype(jnp.int32)
    padded = (cnt + MOE_ROWS - 1) // MOE_ROWS * MOE_ROWS
    pad_end = jnp.cumsum(padded)
    pad_start = pad_end - padded
    dest = pad_start[e] + rank
    nb = -(-(2 * n + N_EXPERTS * (MOE_ROWS - 1)) // MOE_ROWS)
    block_expert = jnp.minimum(
        jnp.searchsorted(pad_end, jnp.arange(nb, dtype=jnp.int32) * MOE_ROWS, side='right'),
        N_EXPERTS - 1).astype(jnp.int32)
    n_used = (pad_end[-1:] // MOE_ROWS).astype(jnp.int32)
    tok = jnp.broadcast_to(jnp.arange(n, dtype=jnp.int32)[:, None], (n, 2))
    row_tok = jnp.zeros((nb * MOE_ROWS,), jnp.int32).at[dest.reshape(-1)].set(tok.reshape(-1))
    xs = xn[row_tok]
    ys = _experts(layer, xs, block_expert, n_used, w_gu, w_dn)
    return y + gate[:, 0:1] * ys[dest[:, 0]].astype(F32) + gate[:, 1:2] * ys[dest[:, 1]].astype(F32)


def _final_norm_body(x_ref, g_ref, o_ref):
    o_ref[...] = _rmsnorm_rows(x_ref[...], g_ref[...])


def _final_norm(x, g):
    n, d = x.shape
    tm = ROW_TILE
    return pl.pallas_call(
        _final_norm_body, grid=(n // tm,),
        in_specs=[pl.BlockSpec((tm, d), lambda i: (i, 0)), pl.BlockSpec((1, d), lambda i: (0, 0))],
        out_specs=pl.BlockSpec((tm, d), lambda i: (i, 0)), out_shape=jax.ShapeDtypeStruct((n, d), F32),
        compiler_params=_cparams("parallel"), name="final_norm",
    )(x, g.reshape(1, d))


def kernel(x_prompt, x_sample, mem_prompt, cache_k, cache_v, cache_mem_k, cache_mem_v, page_table, norm_attn, norm_mem, norm_ffn, norm_final, w_in, w_mem_kv, w_out, diff_lambda, diff_subln, w_router_group, b_router_group, w_router_expert, b_router_expert, w_expert_gu, w_expert_down):
    batch, seq, d = x_prompt.shape
    db, dt, _ = x_sample.shape
    depth = w_in.shape[0]
    n_mem = mem_prompt.shape[1]
    past_len = page_table.shape[1] * cache_k.shape[2]
    n_p, n_s = batch * seq, db * dt
    n = n_p + n_s
    assert n % ROW_TILE == 0 and seq % min(ATTN_TILE, seq) == 0 and seq % min(ROW_TILE, seq) == 0

    x = jnp.concatenate([x_prompt.reshape(n_p, d), x_sample.reshape(n_s, d)], axis=0)
    pos = jnp.concatenate([jnp.tile(jnp.arange(seq, dtype=jnp.int32), batch),
                           jnp.tile(past_len + jnp.arange(dt, dtype=jnp.int32), db)])
    rot_tables = _rotary_tables(pos)

    mem_k, mem_v = _memkv(mem_prompt.reshape(batch * n_mem, d), norm_mem, w_mem_kv)
    mem_k = mem_k.reshape(depth, batch, n_mem, MEM_WIDTH)
    mem_v = mem_v.reshape(depth, batch, n_mem, MEM_WIDTH)

    k_layers, v_layers = [], []
    for i in range(depth):
        is_diff = i % 2 == 1
        diff_params = None
        if is_diff:
            lam_init = 0.8 - 0.6 * math.exp(-0.3 * i)
            diff_params = (diff_lambda[i // 2], diff_subln[i // 2], lam_init)
        kf, vf, qb, kb, vb, qm = _proj(x, norm_attn[i], w_in[i].astype(BF16), rot_tables if is_diff else None)
        k_layers.append(kf)
        v_layers.append(vf)

        o = _prompt_attention(qb, kb, vb, n, batch, seq, diff_params)
        om = _mem_prompt(qm, mem_k[i], mem_v[i], n, batch, seq)
        s3 = lambda a: a[n_p:].reshape(db, dt, a.shape[-1])
        o_s, om_s = _sample_attention(i, s3(qb), s3(kb), s3(vb), s3(qm), cache_k, cache_v,
                                      cache_mem_k, cache_mem_v, page_table, diff_params)
        o = lax.dynamic_update_slice(o, o_s.reshape(n_s, MIX_WIDTH), (n_p, 0))
        om = lax.dynamic_update_slice(om, om_s.reshape(n_s, MEM_WIDTH), (n_p, 0))

        w_route = jnp.zeros((d, LANES), F32)
        w_route = w_route.at[:, :N_GROUPS].set(w_router_group[i])
        w_route = w_route.at[:, ROUTE_LANE0:ROUTE_LANE0 + N_EXPERTS].set(w_router_expert[i])
        wr_hi = w_route.astype(BF16)
        wr_lo = (w_route - wr_hi.astype(F32)).astype(BF16)
        b_route = jnp.zeros((1, LANES), F32)
        b_route = b_route.at[0, :N_GROUPS].set(b_router_group[i])
        b_route = b_route.at[0, ROUTE_LANE0:ROUTE_LANE0 + N_EXPERTS].set(b_router_expert[i].reshape(-1))
        y, xn, info, counts = _out_router(x, o, om, w_out[i].astype(BF16), norm_ffn[i], wr_hi, wr_lo, b_route)
        x = _moe(i, y, xn, info, counts, w_expert_gu, w_expert_down)

    yn = _final_norm(x, norm_final)
    prompt = lambda a: a[:n_p].reshape(batch, seq, a.shape[-1])
    sample = lambda a: a[n_p:].reshape(db, dt, a.shape[-1])
    return (prompt(yn), sample(yn),
            jnp.stack([prompt(a) for a in k_layers]), jnp.stack([prompt(a) for a in v_layers]),
            mem_k, mem_v,
            jnp.stack([sample(a) for a in k_layers]), jnp.stack([sample(a) for a in v_layers]))
```

```python
import functools
import math

import jax
import jax.numpy as jnp
from jax import lax
from jax.experimental import pallas as pl
from jax.experimental.pallas import tpu as pltpu

F32 = jnp.float32
BF16 = jnp.bfloat16

HEAD_DIM = 64
SB_HEADS = 12
DIFF_HEADS = 6
MEM_HEADS = 4
MIX_WIDTH = SB_HEADS * HEAD_DIM
MEM_WIDTH = MEM_HEADS * HEAD_DIM
ROT_DIM = HEAD_DIM // 4
ROPE_THETA = 500000.0
N_GROUPS = 8
EXPERTS_PER_GROUP = 8
N_EXPERTS = N_GROUPS * EXPERTS_PER_GROUP
EXPERT_FF = 256
RMS_EPS = 1e-6
QK_SCALE = HEAD_DIM ** -0.5

LANES = 128
ROW_TILE = 512
ATTN_TILE = 256
DIFF_TILE = 512
LOG2E = 1.4426950408889634
MOE_ROWS = 128
NEG_BIG = -1e30
SB_DEAD = -104.0
VMEM_LIMIT = 48 * 1024 * 1024
ROUTE_LANE0 = N_GROUPS


def _cparams(*sem):
    return pltpu.CompilerParams(dimension_semantics=sem, vmem_limit_bytes=VMEM_LIMIT)


def _rmsnorm_rows(x, g):
    ms = jnp.mean(x * x, axis=-1, keepdims=True)
    return x * lax.rsqrt(ms + RMS_EPS) * g


def _dot(a, b):
    return jnp.dot(a, b, preferred_element_type=F32)


def _dot_nt(a, b):
    return lax.dot_general(a, b, (((1,), (1,)), ((), ())), preferred_element_type=F32)


def _iota(shape, axis):
    return lax.broadcasted_iota(jnp.int32, shape, axis)


def _proj_body(rotary, q_scale, x_ref, g_ref, w_ref, *rest):
    if rotary:
        cos_ref, sa_ref, sb_ref = rest[:3]
        rest = rest[3:]
    kf_ref, vf_ref, qb_ref, kb_ref, vb_ref, qm_ref = rest[-6:]
    xn = _rmsnorm_rows(x_ref[...], g_ref[...]).astype(BF16)
    q = _dot(xn, w_ref[:, 0:MIX_WIDTH])
    k = _dot(xn, w_ref[:, MIX_WIDTH:2 * MIX_WIDTH])
    v = _dot(xn, w_ref[:, 2 * MIX_WIDTH:3 * MIX_WIDTH])
    qm = _dot(xn, w_ref[:, 3 * MIX_WIDTH:])
    if rotary:
        c, sa, sb = cos_ref[...], sa_ref[...], sb_ref[...]

        def rot(t):
            outs = []
            for j in range(MIX_WIDTH // LANES):
                s = t[:, j * LANES:(j + 1) * LANES]
                outs.append(s * c + pltpu.roll(s, LANES - ROT_DIM // 2, 1) * sa
                            + pltpu.roll(s, ROT_DIM // 2, 1) * sb)
            return jnp.concatenate(outs, axis=1)

        q = rot(q)
        k = rot(k)
    kf_ref[...] = k
    vf_ref[...] = v
    qb_ref[...] = (q * q_scale).astype(BF16)
    kb_ref[...] = k.astype(BF16)
    vb_ref[...] = v.astype(BF16)
    qm_ref[...] = (qm * QK_SCALE).astype(BF16)


def _proj(x, row0, n_rows, g, w_bf16, q_scale, rot, layer, depth, kv_prev):
    d = x.shape[1]
    tm = ROW_TILE
    t0 = row0 // tm
    in_specs = [pl.BlockSpec((tm, d), lambda i: (t0 + i, 0)), pl.BlockSpec((1, d), lambda i: (0, 0)),
                pl.BlockSpec(w_bf16.shape, lambda i: (0, 0))]
    args = [x, g.reshape(1, d), w_bf16]
    if rot is not None:
        tables, first, nblk = rot
        in_specs += [pl.BlockSpec((tm, LANES), lambda i: (first + i % nblk, 0))] * 3
        args += list(tables)
    aliases = {}
    if kv_prev is not None:
        aliases = {len(args): 0, len(args) + 1: 1}
        in_specs += [pl.BlockSpec(memory_space=pl.ANY)] * 2
        args += list(kv_prev)
    row = lambda w: pl.BlockSpec((tm, w), lambda i: (i, 0))
    kv_spec = pl.BlockSpec((None, tm, MIX_WIDTH), lambda i: (layer, i, 0))
    out_shape = [jax.ShapeDtypeStruct((depth, n_rows, MIX_WIDTH), F32)] * 2 + \
                [jax.ShapeDtypeStruct((n_rows, MIX_WIDTH), BF16)] * 3 + \
                [jax.ShapeDtypeStruct((n_rows, MEM_WIDTH), BF16)]
    out_specs = [kv_spec] * 2 + [row(MIX_WIDTH)] * 3 + [row(MEM_WIDTH)]
    return pl.pallas_call(
        functools.partial(_proj_body, rot is not None, q_scale),
        grid=(n_rows // tm,), in_specs=in_specs, out_specs=out_specs, out_shape=out_shape,
        input_output_aliases=aliases,
        compiler_params=_cparams("parallel"), name="proj_rot" if rot is not None else "proj",
    )(*args)


def _rotary_tables(pos):
    half = ROT_DIM // 2
    lane = jnp.arange(LANES) % HEAD_DIM
    inv_freq = 1.0 / (ROPE_THETA ** ((lane % half).astype(F32) * 2.0 / ROT_DIM))
    ang = pos.astype(F32)[:, None] * inv_freq[None, :]
    cos, sin = jnp.cos(ang), jnp.sin(ang)
    c = jnp.where(lane[None, :] < ROT_DIM, cos, 1.0)
    sa = jnp.where(lane[None, :] < half, -sin, 0.0)
    sb = jnp.where((lane[None, :] >= half) & (lane[None, :] < ROT_DIM), sin, 0.0)
    return c, sa, sb


def _memkv_body(mem_ref, g_ref, w_ref, k_ref, v_ref):
    xn = _rmsnorm_rows(mem_ref[...], g_ref[...]).astype(BF16)
    kv = _dot(xn, w_ref[...].astype(BF16))
    k_ref[...] = kv[:, :MEM_WIDTH]
    v_ref[...] = kv[:, MEM_WIDTH:]


def _memkv(mem2d, norm_mem, w_mem_kv):
    depth, d = norm_mem.shape
    m = mem2d.shape[0]
    out = jax.ShapeDtypeStruct((depth, m, MEM_WIDTH), F32)
    ospec = pl.BlockSpec((None, m, MEM_WIDTH), lambda l: (l, 0, 0))
    return pl.pallas_call(
        _memkv_body, grid=(depth,),
        in_specs=[pl.BlockSpec((m, d), lambda l: (0, 0)),
                  pl.BlockSpec((None, 1, d), lambda l: (l, 0, 0)),
                  pl.BlockSpec((None, d, 2 * MEM_WIDTH), lambda l: (l, 0, 0))],
        out_specs=[ospec, ospec], out_shape=[out, out],
        compiler_params=_cparams("parallel"), name="mem_kv",
    )(mem2d, norm_mem.reshape(depth, 1, d), w_mem_kv)


def _neg_upper(tk):
    return jnp.where(_iota((tk, tk), 0) > _iota((tk, tk), 1), -1.0, 0.0).astype(BF16)


def _sb_chunk(qh, ks, vs, uneg, carry, acc, mask):
    z = _dot_nt(qh, ks)
    sp = jnp.maximum(z, 0.0) + jnp.log(1.0 + jnp.exp(-jnp.abs(z)))
    spm = sp if mask is None else jnp.where(mask, sp, 0.0)
    hi = spm.astype(BF16)
    lo = (spm - hi.astype(F32)).astype(BF16)
    between = _dot(hi, uneg) + _dot(lo, uneg) + carry
    a = jnp.exp(z - sp + between)
    if mask is not None:
        a = jnp.where(mask, a, 0.0)
    acc = acc + _dot(a.astype(BF16), vs)
    carry = carry - jnp.sum(spm, axis=-1, keepdims=True)
    return carry, acc


def _sm_chunk(qh, ks, vs, m, l, acc, mask):
    s = _dot_nt(qh, ks)
    if mask is not None:
        s = jnp.where(mask, s, NEG_BIG)
    m_new = jnp.maximum(m, jnp.max(s, axis=-1, keepdims=True))
    alpha = jnp.exp2(m - m_new)
    p = jnp.exp2(s - jnp.concatenate([m_new] * (s.shape[1] // LANES), axis=1))
    l = alpha * l + jnp.sum(p, axis=-1, keepdims=True)
    acc = alpha * acc + _dot(p.astype(BF16), vs)
    return m_new, l, acc


def _diff_lambda(lam_ref, lam_init):
    lv = lam_ref[...]
    s1 = jnp.sum(lv[0:1] * lv[1:2], axis=-1, keepdims=True)
    s2 = jnp.sum(lv[2:3] * lv[3:4], axis=-1, keepdims=True)
    return jnp.exp(s1) - jnp.exp(s2) + lam_init


def _head_rmsnorm(o, gain, lam_init):
    return _rmsnorm_rows(o, gain) * (1.0 - lam_init)


def _mem_attention(qbd, mk, mv):
    s = _dot_nt(qbd, mk)
    p = jnp.exp(s - jnp.max(s, axis=-1, keepdims=True))
    o = _dot(p.astype(BF16), mv)
    return o / jnp.sum(p, axis=-1, keepdims=True)


def _sb_prompt_body(q_ref, k_ref, v_ref, o_ref, acc_ref, carry_ref, *, t):
    i = pl.program_id(2)
    q = q_ref[...].astype(F32)
    lane = _iota((t, LANES), 1)
    qhs = (jnp.where(lane < HEAD_DIM, q, 0.0).astype(BF16), jnp.where(lane >= HEAD_DIM, q, 0.0).astype(BF16))
    uneg = _neg_upper(t)
    causal = _iota((t, t), 1) < _iota((t, t), 0)

    def process(c, mask):
        start = pl.multiple_of(c * t, t)
        ks = k_ref[pl.ds(start, t), :]
        vs = v_ref[pl.ds(start, t), :]
        for h in range(2):
            carry, acc = _sb_chunk(qhs[h], ks, vs, uneg, carry_ref[h], acc_ref[h], mask)
            carry_ref[h] = carry
            acc_ref[h] = acc

    def alive():
        return (jnp.max(carry_ref[...]) > SB_DEAD).astype(jnp.int32)

    carry_ref[...] = jnp.zeros_like(carry_ref)
    acc_ref[...] = jnp.zeros_like(acc_ref)

    @pl.when(i == 0)
    def _():
        process(i, causal)

    @pl.when(i > 0)
    def _():
        process(i, causal)
        process(i - 1, None)

    def body(st):
        process(st[0], None)
        return st[0] - 1, alive()

    lax.while_loop(lambda st: (st[0] >= 0) & (st[1] > 0), body, (i - 2, alive()))
    o_ref[...] = jnp.where(lane < HEAD_DIM, acc_ref[0], acc_ref[1]).astype(BF16)


def _diff_prompt_body(lam_ref, gain_ref, q_ref, k_ref, v_ref, o_ref, m_ref, l_ref, acc_ref, *, t, lam_init):
    i = pl.program_id(2)
    q = q_ref[...].astype(F32)
    lane = _iota((t, LANES), 1)
    qhs = (jnp.where(lane < HEAD_DIM, q, 0.0).astype(BF16), jnp.where(lane >= HEAD_DIM, q, 0.0).astype(BF16))

    def process(c, mask):
        start = pl.multiple_of(c * t, t)
        ks = k_ref[pl.ds(start, t), :]
        vs = v_ref[pl.ds(start, t), :]
        for h in range(2):
            m, l, acc = _sm_chunk(qhs[h], ks, vs, m_ref[h], l_ref[h], acc_ref[h], mask)
            m_ref[h] = m
            l_ref[h] = l
            acc_ref[h] = acc

    m_ref[...] = jnp.full_like(m_ref, NEG_BIG)
    l_ref[...] = jnp.zeros_like(l_ref)
    acc_ref[...] = jnp.zeros_like(acc_ref)

    def body(c, carry):
        process(c, None)
        return carry

    lax.fori_loop(0, i, body, 0)
    process(i, _iota((t, t), 1) <= _iota((t, t), 0))
    lam = _diff_lambda(lam_ref, lam_init)
    o = acc_ref[0] / l_ref[0] - lam * (acc_ref[1] / l_ref[1])
    o_ref[...] = _head_rmsnorm(o, gain_ref[...], lam_init).astype(BF16)


def _prompt_attention(qb, kb, vb, n_rows, batch, seq, diff_params):
    t = min(ATTN_TILE if diff_params is None else DIFF_TILE, seq)
    nq = seq // t
    nh = MIX_WIDTH // LANES
    qspec = pl.BlockSpec((t, LANES), lambda b, h, i: (b * nq + i, h))
    kvspec = pl.BlockSpec((seq, LANES), lambda b, h, i: (b, h))
    out_shape = jax.ShapeDtypeStruct((n_rows, MIX_WIDTH), BF16)
    if diff_params is None:
        body = functools.partial(_sb_prompt_body, t=t)
        in_specs, args = [qspec, kvspec, kvspec], [qb, kb, vb]
        scratch = [pltpu.VMEM((2, t, LANES), F32), pltpu.VMEM((2, t, 1), F32)]
        name = "sb_prompt"
    else:
        lam_vecs, gain, lam_init = diff_params
        body = functools.partial(_diff_prompt_body, t=t, lam_init=lam_init)
        in_specs = [pl.BlockSpec((4, HEAD_DIM), lambda b, h, i: (0, 0)),
                    pl.BlockSpec((1, LANES), lambda b, h, i: (0, 0)), qspec, kvspec, kvspec]
        args = [lam_vecs, gain.reshape(1, LANES), qb, kb, vb]
        scratch = [pltpu.VMEM((2, t, LANES), F32)] * 3
        name = "diff_prompt"
    return pl.pallas_call(
        body, grid=(batch, nh, nq), in_specs=in_specs, out_specs=qspec, out_shape=out_shape,
        scratch_shapes=scratch, compiler_params=_cparams("parallel", "parallel", "arbitrary"), name=name,
    )(*args)


def _mem_prompt_body(q_ref, mk_ref, mv_ref, o_ref):
    q = q_ref[...].astype(F32)
    mk = mk_ref[...].astype(BF16)
    mv = mv_ref[...].astype(BF16)
    lane_head = _iota(q.shape, 1) // HEAD_DIM
    out = jnp.zeros(q.shape, F32)
    for h in range(MEM_HEADS):
        o = _mem_attention(jnp.where(lane_head == h, q, 0.0).astype(BF16), mk, mv)
        out = jnp.where(lane_head == h, o, out)
    o_ref[...] = out.astype(BF16)


def _mem_prompt(qm, mk, mv, n_rows, batch, seq):
    t = min(ROW_TILE, seq)
    nq = seq // t
    n_mem = mk.shape[1]
    qspec = pl.BlockSpec((t, MEM_WIDTH), lambda b, i: (b * nq + i, 0))
    mspec = pl.BlockSpec((None, n_mem, MEM_WIDTH), lambda b, i: (b, 0, 0))
    return pl.pallas_call(
        _mem_prompt_body, grid=(batch, nq), in_specs=[qspec, mspec, mspec], out_specs=qspec,
        out_shape=jax.ShapeDtypeStruct((n_rows, MEM_WIDTH), BF16),
        compiler_params=_cparams("parallel", "parallel"), name="mem_prompt",
    )(qm, mk, mv)


def _block_diag_queries(q, n_blocks, block_lanes):
    dt = q.shape[0]
    qt = jnp.concatenate([q] * n_blocks, axis=0)
    keep = (_iota(qt.shape, 0) // dt) == (_iota(qt.shape, 1) // block_lanes)
    return jnp.where(keep, qt, 0.0).astype(BF16)


def _fold_blocks(full, dt, block_lanes):
    keep = (_iota(full.shape, 0) // dt) == (_iota(full.shape, 1) // block_lanes)
    sel = jnp.where(keep, full, 0.0)
    out = sel[0:dt]
    for j in range(1, full.shape[0] // dt):
        out = out + sel[j * dt:(j + 1) * dt]
    return out


def _sample_body(pt_ref, *refs, n_pages, page, dt, diff, lam_init):
    del pt_ref
    if diff:
        lam_ref, gain_ref = refs[:2]
        refs = refs[2:]
    q_ref, kn_ref, vn_ref, qm_ref, mk_ref, mv_ref = refs[:6]
    k_pages = refs[6:6 + n_pages]
    v_pages = refs[6 + n_pages:6 + 2 * n_pages]
    o_ref, om_ref, kall, vall = refs[6 + 2 * n_pages:]

    n_maps = MIX_WIDTH // HEAD_DIM
    rows = n_maps * dt
    past = n_pages * page
    qbd = _block_diag_queries(q_ref[...].astype(F32), n_maps, HEAD_DIM)
    for p in range(n_pages):
        kall[p * page:(p + 1) * page, :] = k_pages[p][...].astype(BF16)
        vall[p * page:(p + 1) * page, :] = v_pages[p][...].astype(BF16)
    pad = jnp.zeros((page - dt, MIX_WIDTH), F32)
    kall[past:, :] = jnp.concatenate([kn_ref[...].astype(F32), pad], axis=0).astype(BF16)
    vall[past:, :] = jnp.concatenate([vn_ref[...].astype(F32), pad], axis=0).astype(BF16)
    tpos = _iota((rows, page), 0) % dt
    col = _iota((rows, page), 1)
    z = _dot_nt(qbd, kall[...])
    chunks = [z[:, c * page:(c + 1) * page] for c in range(n_pages + 1)]

    if not diff:
        new_ok = col < tpos
        sp = [jnp.maximum(zc, 0.0) + jnp.log(1.0 + jnp.exp(-jnp.abs(zc))) for zc in chunks]
        spm = sp[:-1] + [jnp.where(new_ok, sp[-1], 0.0)]
        stacked = jnp.concatenate(spm, axis=0)
        hi = stacked.astype(BF16)
        lo = (stacked - hi.astype(F32)).astype(BF16)
        uneg = _neg_upper(page)
        later = _dot(hi, uneg) + _dot(lo, uneg)
        carry = jnp.zeros((rows, 1), F32)
        a = [None] * (n_pages + 1)
        for c in reversed(range(n_pages + 1)):
            a[c] = jnp.exp(chunks[c] - sp[c] + later[c * rows:(c + 1) * rows] + carry)
            carry = carry - jnp.sum(spm[c], axis=-1, keepdims=True)
        a[-1] = jnp.where(new_ok, a[-1], 0.0)
        acc = _dot(jnp.concatenate(a, axis=1).astype(BF16), vall[...])
        o = _fold_blocks(acc, dt, HEAD_DIM)
    else:
        s = jnp.concatenate(chunks[:-1] + [jnp.where(col <= tpos, chunks[-1], NEG_BIG)], axis=1)
        p = jnp.exp2(s - jnp.max(s, axis=-1, keepdims=True))
        l = jnp.sum(p, axis=-1, keepdims=True)
        acc = _dot(p.astype(BF16), vall[...])
        lam = _diff_lambda(lam_ref, lam_init)
        second_map = ((_iota((rows, 1), 0) // dt) % 2) == 1
        weight = jnp.where(second_map, -lam, 1.0) / l
        o = _fold_blocks(acc * weight, 2 * dt, 2 * HEAD_DIM)
        o = o[0:dt] + o[dt:2 * dt]
        gain = gain_ref[...]
        o = jnp.concatenate(
            [_head_rmsnorm(o[:, j * LANES:(j + 1) * LANES], gain, lam_init) for j in range(DIFF_HEADS)], axis=1)
    o_ref[...] = o.astype(BF16)

    qmbd = _block_diag_queries(qm_ref[...].astype(F32), MEM_HEADS, HEAD_DIM)
    om = _mem_attention(qmbd, mk_ref[...].astype(BF16), mv_ref[...].astype(BF16))
    om_ref[...] = _fold_blocks(om, dt, HEAD_DIM).astype(BF16)


def _sample_attention(layer, q3, kn3, vn3, qm3, cache_k, cache_v, cache_mem_k, cache_mem_v, page_table,
                      diff_params):
    db, dt, _ = q3.shape
    n_pages = page_table.shape[1]
    page = cache_k.shape[2]
    n_mem = cache_mem_k.shape[2]
    diff = diff_params is not None

    def req(w):
        return pl.BlockSpec((None, dt, w), lambda b, pt: (b, 0, 0))

    def page_spec(p):
        return pl.BlockSpec((None, None, page, MIX_WIDTH), lambda b, pt: (layer, pt[b * n_pages + p], 0, 0))

    mem_spec = pl.BlockSpec((None, None, n_mem, MEM_WIDTH), lambda b, pt: (layer, b, 0, 0))
    in_specs = [req(MIX_WIDTH)] * 3 + [req(MEM_WIDTH), mem_spec, mem_spec]
    in_specs += [page_spec(p) for p in range(n_pages)] * 2
    args = [q3, kn3, vn3, qm3, cache_mem_k, cache_mem_v] + [cache_k] * n_pages + [cache_v] * n_pages
    lam_init = 0.0
    if diff:
        lam_vecs, gain, lam_init = diff_params
        in_specs = [pl.BlockSpec((4, HEAD_DIM), lambda b, pt: (0, 0)),
                    pl.BlockSpec((1, LANES), lambda b, pt: (0, 0))] + in_specs
        args = [lam_vecs, gain.reshape(1, LANES)] + args
    body = functools.partial(_sample_body, n_pages=n_pages, page=page, dt=dt, diff=diff, lam_init=lam_init)
    return pl.pallas_call(
        body,
        grid_spec=pltpu.PrefetchScalarGridSpec(
            num_scalar_prefetch=1, grid=(db,), in_specs=in_specs,
            out_specs=[req(MIX_WIDTH), req(MEM_WIDTH)],
            scratch_shapes=[pltpu.VMEM(((n_pages + 1) * page, MIX_WIDTH), BF16)] * 2),
        out_shape=[jax.ShapeDtypeStruct((db, dt, MIX_WIDTH), BF16),
                   jax.ShapeDtypeStruct((db, dt, MEM_WIDTH), BF16)],
        compiler_params=_cparams("arbitrary"), name="diff_sample" if diff else "sb_sample",
    )(page_table.reshape(-1), *args)


def _out_router_body(x_ref, o_ref, om_ref, wo_ref, g_ref, wrh_ref, wrl_ref, br_ref,
                     y_ref, xn_ref, info_ref, cnt_ref, run_ref):
    @pl.when(pl.program_id(0) == 0)
    def _():
        run_ref[...] = jnp.zeros_like(run_ref)

    y = x_ref[...] + _dot(o_ref[...], wo_ref[0:MIX_WIDTH, :]) + _dot(om_ref[...], wo_ref[MIX_WIDTH:, :])
    y_ref[...] = y
    xn = _rmsnorm_rows(y, g_ref[...])
    xh = xn.astype(BF16)
    xn_ref[...] = xh
    xl = (xn - xh.astype(F32)).astype(BF16)
    wh, wl = wrh_ref[...], wrl_ref[...]
    logit = _dot(xh, wh) + _dot(xl, wh) + _dot(xh, wl) + br_ref[...]

    tm = logit.shape[0]
    lane = _iota((tm, LANES), 1)
    lg = jnp.where(lane < N_GROUPS, logit, -jnp.inf)
    mg = jnp.max(lg, axis=-1, keepdims=True)
    g_idx = jnp.min(jnp.where(lg == mg, lane, LANES), axis=-1, keepdims=True)
    g_w = 1.0 / jnp.sum(jnp.exp(lg - mg), axis=-1, keepdims=True)
    in_group = (lane >= ROUTE_LANE0) & (((lane - ROUTE_LANE0) >> 3) == g_idx) & (lane < ROUTE_LANE0 + N_EXPERTS)
    le = jnp.where(in_group, logit, -jnp.inf)
    v1 = jnp.max(le, axis=-1, keepdims=True)
    i1 = jnp.min(jnp.where(le == v1, lane, LANES), axis=-1, keepdims=True)
    le2 = jnp.where(lane == i1, -jnp.inf, le)
    v2 = jnp.max(le2, axis=-1, keepdims=True)
    i2 = jnp.min(jnp.where(le2 == v2, lane, LANES), axis=-1, keepdims=True)
    e21 = jnp.exp(v2 - v1)
    w1 = 1.0 / (1.0 + e21)
    w2 = e21 * w1

    hit1, hit2 = lane == i1, lane == i2
    onehot = jnp.where(hit1 | hit2, 1.0, 0.0)
    lower = jnp.where(_iota((tm, tm), 1) < _iota((tm, tm), 0), 1.0, 0.0).astype(BF16)
    before = _dot(lower, onehot.astype(BF16)) + run_ref[...]
    r1 = jnp.sum(jnp.where(hit1, before, 0.0), axis=-1, keepdims=True)
    r2 = jnp.sum(jnp.where(hit2, before, 0.0), axis=-1, keepdims=True)
    run_ref[...] = run_ref[...] + jnp.sum(onehot, axis=0, keepdims=True)
    cnt_ref[...] = run_ref[...]

    info = jnp.zeros((tm, LANES), F32)
    fields = ((i1 - ROUTE_LANE0).astype(F32), (i2 - ROUTE_LANE0).astype(F32), g_w * w1, g_w * w2, r1, r2)
    for j, val in enumerate(fields):
        info = jnp.where(lane == j, val, info)
    info_ref[...] = info


def _out_router(x, o, om, wo_bf16, g_ffn, wr_hi, wr_lo, b_route):
    n, d = x.shape
    tm = ROW_TILE
    row = lambda w: pl.BlockSpec((tm, w), lambda i: (i, 0))
    full = lambda a: pl.BlockSpec(a.shape, lambda i: (0, 0))
    g2 = g_ffn.reshape(1, d)
    return pl.pallas_call(
        _out_router_body, grid=(n // tm,),
        in_specs=[row(d), row(MIX_WIDTH), row(MEM_WIDTH), full(wo_bf16), full(g2), full(wr_hi), full(wr_lo),
                  full(b_route)],
        out_specs=[row(d), row(d), row(LANES), pl.BlockSpec((1, LANES), lambda i: (0, 0))],
        out_shape=[jax.ShapeDtypeStruct((n, d), F32), jax.ShapeDtypeStruct((n, d), BF16),
                   jax.ShapeDtypeStruct((n, LANES), F32), jax.ShapeDtypeStruct((1, LANES), F32)],
        scratch_shapes=[pltpu.VMEM((1, LANES), F32)],
        compiler_params=_cparams("arbitrary"), name="out_router",
    )(x, o, om, wo_bf16, g2, wr_hi, wr_lo, b_route)


def _expert_body(be_ref, nused_ref, x_ref, wgu_ref, wdn_ref, y_ref, wgu_s, wdn_s):
    i = pl.program_id(0)
    prev = be_ref[jnp.maximum(i - 1, 0)]

    @pl.when((i == 0) | (be_ref[i] != prev))
    def _():
        wgu_s[...] = wgu_ref[...].astype(BF16)
        wdn_s[...] = wdn_ref[...].astype(BF16)

    @pl.when(i < nused_ref[0])
    def _():
        h = _dot(x_ref[...], wgu_s[...])
        a, u = h[:, :EXPERT_FF], h[:, EXPERT_FF:]
        act = a * (1.0 / (1.0 + jnp.exp(-a))) * u
        y_ref[...] = _dot(act.astype(BF16), wdn_s[...]).astype(BF16)

    @pl.when(i >= nused_ref[0])
    def _():
        y_ref[...] = jnp.zeros_like(y_ref)


def _experts(layer, xs, block_expert, n_used, w_gu, w_dn):
    rows, d = xs.shape
    nb = rows // MOE_ROWS
    return pl.pallas_call(
        _expert_body,
        grid_spec=pltpu.PrefetchScalarGridSpec(
            num_scalar_prefetch=2, grid=(nb,),
            in_specs=[pl.BlockSpec((MOE_ROWS, d), lambda i, be, nu: (i, 0)),
                      pl.BlockSpec((None, None, d, 2 * EXPERT_FF), lambda i, be, nu: (layer, be[i], 0, 0)),
                      pl.BlockSpec((None, None, EXPERT_FF, d), lambda i, be, nu: (layer, be[i], 0, 0))],
            out_specs=pl.BlockSpec((MOE_ROWS, d), lambda i, be, nu: (i, 0)),
            scratch_shapes=[pltpu.VMEM((d, 2 * EXPERT_FF), BF16), pltpu.VMEM((EXPERT_FF, d), BF16)]),
        out_shape=jax.ShapeDtypeStruct((rows, d), BF16),
        compiler_params=_cparams("arbitrary"), name="experts",
    )(block_expert, n_used, xs, w_gu, w_dn)


def _moe(layer, y, xn, info, counts, w_gu, w_dn):
    n = y.shape[0]
    e = info[:, 0:2].astype(jnp.int32)
    gate = info[:, 2:4]
    rank = info[:, 4:6].astype(jnp.int32)
    cnt = counts[0, ROUTE_LANE0:ROUTE_LANE0 + N_EXPERTS].astype(jnp.int32)
    padded = (cnt + MOE_ROWS - 1) // MOE_ROWS * MOE_ROWS
    pad_end = jnp.cumsum(padded)
    pad_start = pad_end - padded
    dest = pad_start[e] + rank
    nb = -(-(2 * n + N_EXPERTS * (MOE_ROWS - 1)) // MOE_ROWS)
    block_row0 = jnp.arange(nb, dtype=jnp.int32) * MOE_ROWS
    block_expert = jnp.minimum(jnp.sum((pad_end[None, :] <= block_row0[:, None]).astype(jnp.int32), axis=1),
                               N_EXPERTS - 1)
    n_used = (pad_end[-1:] // MOE_ROWS).astype(jnp.int32)
    tok = jnp.broadcast_to(jnp.arange(n, dtype=jnp.int32)[:, None], (n, 2))
    row_tok = jnp.zeros((nb * MOE_ROWS,), jnp.int32).at[dest.reshape(-1)].set(tok.reshape(-1))
    xs = xn[row_tok]
    ys = _experts(layer, xs, block_expert, n_used, w_gu, w_dn)
    return y + gate[:, 0:1] * ys[dest[:, 0]].astype(F32) + gate[:, 1:2] * ys[dest[:, 1]].astype(F32)


def _final_norm_body(x_ref, g_ref, o_ref):
    o_ref[...] = _rmsnorm_rows(x_ref[...], g_ref[...])


def _final_norm(x, row0, n_rows, g):
    d = x.shape[1]
    tm = ROW_TILE
    t0 = row0 // tm
    return pl.pallas_call(
        _final_norm_body, grid=(n_rows // tm,),
        in_specs=[pl.BlockSpec((tm, d), lambda i: (t0 + i, 0)), pl.BlockSpec((1, d), lambda i: (0, 0))],
        out_specs=pl.BlockSpec((tm, d), lambda i: (i, 0)), out_shape=jax.ShapeDtypeStruct((n_rows, d), F32),
        compiler_params=_cparams("parallel"), name="final_norm",
    )(x, g.reshape(1, d))


def kernel(x_prompt, x_sample, mem_prompt, cache_k, cache_v, cache_mem_k, cache_mem_v, page_table, norm_attn, norm_mem, norm_ffn, norm_final, w_in, w_mem_kv, w_out, diff_lambda, diff_subln, w_router_group, b_router_group, w_router_expert, b_router_expert, w_expert_gu, w_expert_down):
    batch, seq, d = x_prompt.shape
    db, dt, _ = x_sample.shape
    depth = w_in.shape[0]
    n_mem = mem_prompt.shape[1]
    past_len = page_table.shape[1] * cache_k.shape[2]
    n_p, n_s = batch * seq, db * dt
    n = n_p + n_s
    assert seq % ROW_TILE == 0 and n_s % ROW_TILE == 0 and ROW_TILE % dt == 0 and seq % DIFF_TILE == 0

    x = jnp.concatenate([x_prompt.reshape(n_p, d), x_sample.reshape(n_s, d)], axis=0)
    seq_tiles = seq // ROW_TILE
    pos = jnp.concatenate([jnp.arange(seq, dtype=jnp.int32),
                           past_len + jnp.arange(ROW_TILE, dtype=jnp.int32) % dt])
    rot_tables = _rotary_tables(pos)

    mem_k, mem_v = _memkv(mem_prompt.reshape(batch * n_mem, d), norm_mem, w_mem_kv)
    mem_k = mem_k.reshape(depth, batch, n_mem, MEM_WIDTH)
    mem_v = mem_v.reshape(depth, batch, n_mem, MEM_WIDTH)

    kv_p = kv_s = None
    for i in range(depth):
        is_diff = i % 2 == 1
        diff_params = None
        q_scale = QK_SCALE
        if is_diff:
            lam_init = 0.8 - 0.6 * math.exp(-0.3 * i)
            diff_params = (diff_lambda[i // 2], diff_subln[i // 2], lam_init)
            q_scale = QK_SCALE * LOG2E
        w_in_i = w_in[i].astype(BF16)
        rot_p = (rot_tables, 0, seq_tiles) if is_diff else None
        rot_s = (rot_tables, seq_tiles, 1) if is_diff else None
        *kv_p, qb, kb, vb, qm = _proj(x, 0, n_p, norm_attn[i], w_in_i, q_scale, rot_p, i, depth, kv_p)
        *kv_s, qb_s, kb_s, vb_s, qm_s = _proj(x, n_p, n_s, norm_attn[i], w_in_i, q_scale, rot_s, i, depth, kv_s)

        o = _prompt_attention(qb, kb, vb, n, batch, seq, diff_params)
        om = _mem_prompt(qm, mem_k[i], mem_v[i], n, batch, seq)
        s3 = lambda a: a.reshape(db, dt, a.shape[-1])
        o_s, om_s = _sample_attention(i, s3(qb_s), s3(kb_s), s3(vb_s), s3(qm_s), cache_k, cache_v,
                                      cache_mem_k, cache_mem_v, page_table, diff_params)
        o = lax.dynamic_update_slice(o, o_s.reshape(n_s, MIX_WIDTH), (n_p, 0))
        om = lax.dynamic_update_slice(om, om_s.reshape(n_s, MEM_WIDTH), (n_p, 0))

        w_route = jnp.zeros((d, LANES), F32)
        w_route = w_route.at[:, :N_GROUPS].set(w_router_group[i])
        w_route = w_route.at[:, ROUTE_LANE0:ROUTE_LANE0 + N_EXPERTS].set(w_router_expert[i])
        wr_hi = w_route.astype(BF16)
        wr_lo = (w_route - wr_hi.astype(F32)).astype(BF16)
        b_route = jnp.zeros((1, LANES), F32)
        b_route = b_route.at[0, :N_GROUPS].set(b_router_group[i])
        b_route = b_route.at[0, ROUTE_LANE0:ROUTE_LANE0 + N_EXPERTS].set(b_router_expert[i].reshape(-1))
        y, xn, info, counts = _out_router(x, o, om, w_out[i].astype(BF16), norm_ffn[i], wr_hi, wr_lo, b_route)
        x = _moe(i, y, xn, info, counts, w_expert_gu, w_expert_down)

    y_p = _final_norm(x, 0, n_p, norm_final).reshape(batch, seq, d)
    y_s = _final_norm(x, n_p, n_s, norm_final).reshape(db, dt, d)
    prompt = lambda a: a.reshape(depth, batch, seq, MIX_WIDTH)
    sample = lambda a: a.reshape(depth, db, dt, MIX_WIDTH)
    return (y_p, y_s, prompt(kv_p[0]), prompt(kv_p[1]), mem_k, mem_v, sample(kv_s[0]), sample(kv_s[1]))
```

```python
import functools
import math

import jax
import jax.numpy as jnp
from jax import lax
from jax.experimental import pallas as pl
from jax.experimental.pallas import tpu as pltpu

F32 = jnp.float32
BF16 = jnp.bfloat16

HEAD_DIM = 64
SB_HEADS = 12
DIFF_HEADS = 6
MEM_HEADS = 4
MIX_WIDTH = SB_HEADS * HEAD_DIM
MEM_WIDTH = MEM_HEADS * HEAD_DIM
ROT_DIM = HEAD_DIM // 4
ROPE_THETA = 500000.0
N_GROUPS = 8
EXPERTS_PER_GROUP = 8
N_EXPERTS = N_GROUPS * EXPERTS_PER_GROUP
EXPERT_FF = 256
RMS_EPS = 1e-6
QK_SCALE = HEAD_DIM ** -0.5

LANES = 128
ROW_TILE = 512
ATTN_TILE = 256
DIFF_TILE = 512
LOG2E = 1.4426950408889634
MOE_ROWS = 128
MOVE_ROWS = 1024
NEG_BIG = -1e30
SB_DEAD = -104.0
VMEM_LIMIT = 48 * 1024 * 1024
ROUTE_LANE0 = N_GROUPS


def _cparams(*sem):
    return pltpu.CompilerParams(dimension_semantics=sem, vmem_limit_bytes=VMEM_LIMIT)


def _rmsnorm_rows(x, g):
    ms = jnp.mean(x * x, axis=-1, keepdims=True)
    return x * lax.rsqrt(ms + RMS_EPS) * g


def _dot(a, b):
    return jnp.dot(a, b, preferred_element_type=F32)


def _dot_nt(a, b):
    return lax.dot_general(a, b, (((1,), (1,)), ((), ())), preferred_element_type=F32)


def _iota(shape, axis):
    return lax.broadcasted_iota(jnp.int32, shape, axis)


def _proj_body(rotary, q_scale, x_ref, g_ref, w_ref, *rest):
    if rotary:
        cos_ref, sa_ref, sb_ref = rest[:3]
        rest = rest[3:]
    kf_ref, vf_ref, qb_ref, kb_ref, vb_ref, qm_ref = rest[-6:]
    xn = _rmsnorm_rows(x_ref[...], g_ref[...]).astype(BF16)
    q = _dot(xn, w_ref[:, 0:MIX_WIDTH])
    k = _dot(xn, w_ref[:, MIX_WIDTH:2 * MIX_WIDTH])
    v = _dot(xn, w_ref[:, 2 * MIX_WIDTH:3 * MIX_WIDTH])
    qm = _dot(xn, w_ref[:, 3 * MIX_WIDTH:])
    if rotary:
        c, sa, sb = cos_ref[...], sa_ref[...], sb_ref[...]

        def rot(t):
            outs = []
            for j in range(MIX_WIDTH // LANES):
                s = t[:, j * LANES:(j + 1) * LANES]
                outs.append(s * c + pltpu.roll(s, LANES - ROT_DIM // 2, 1) * sa
                            + pltpu.roll(s, ROT_DIM // 2, 1) * sb)
            return jnp.concatenate(outs, axis=1)

        q = rot(q)
        k = rot(k)
    kf_ref[...] = k
    vf_ref[...] = v
    qb_ref[...] = (q * q_scale).astype(BF16)
    kb_ref[...] = k.astype(BF16)
    vb_ref[...] = v.astype(BF16)
    qm_ref[...] = (qm * QK_SCALE).astype(BF16)


def _proj(x, row0, n_rows, g, w_bf16, q_scale, rot, layer, kv_prev):
    d = x.shape[1]
    tm = ROW_TILE
    t0 = row0 // tm
    in_specs = [pl.BlockSpec((tm, d), lambda i: (t0 + i, 0)), pl.BlockSpec((1, d), lambda i: (0, 0)),
                pl.BlockSpec(w_bf16.shape, lambda i: (0, 0))]
    args = [x, g.reshape(1, d), w_bf16]
    if rot is not None:
        tables, first, nblk = rot
        in_specs += [pl.BlockSpec((tm, LANES), lambda i: (first + i % nblk, 0))] * 3
        args += list(tables)
    aliases = {len(args): 0, len(args) + 1: 1}
    in_specs += [pl.BlockSpec(memory_space=pl.ANY)] * 2
    args += list(kv_prev)
    row = lambda w: pl.BlockSpec((tm, w), lambda i: (i, 0))
    kv_spec = pl.BlockSpec((None, tm, MIX_WIDTH), lambda i: (layer, i, 0))
    out_shape = [jax.ShapeDtypeStruct(kv_prev[0].shape, F32)] * 2 + \
                [jax.ShapeDtypeStruct((n_rows, MIX_WIDTH), BF16)] * 3 + \
                [jax.ShapeDtypeStruct((n_rows, MEM_WIDTH), BF16)]
    out_specs = [kv_spec] * 2 + [row(MIX_WIDTH)] * 3 + [row(MEM_WIDTH)]
    return pl.pallas_call(
        functools.partial(_proj_body, rot is not None, q_scale),
        grid=(n_rows // tm,), in_specs=in_specs, out_specs=out_specs, out_shape=out_shape,
        input_output_aliases=aliases,
        compiler_params=_cparams("parallel"), name="proj_rot" if rot is not None else "proj",
    )(*args)


def _rotary_tables(pos):
    half = ROT_DIM // 2
    lane = jnp.arange(LANES) % HEAD_DIM
    inv_freq = 1.0 / (ROPE_THETA ** ((lane % half).astype(F32) * 2.0 / ROT_DIM))
    ang = pos.astype(F32)[:, None] * inv_freq[None, :]
    cos, sin = jnp.cos(ang), jnp.sin(ang)
    c = jnp.where(lane[None, :] < ROT_DIM, cos, 1.0)
    sa = jnp.where(lane[None, :] < half, -sin, 0.0)
    sb = jnp.where((lane[None, :] >= half) & (lane[None, :] < ROT_DIM), sin, 0.0)
    return c, sa, sb


def _memkv_body(mem_ref, g_ref, w_ref, k_ref, v_ref):
    xn = _rmsnorm_rows(mem_ref[...], g_ref[...]).astype(BF16)
    kv = _dot(xn, w_ref[...].astype(BF16))
    k_ref[...] = kv[:, :MEM_WIDTH]
    v_ref[...] = kv[:, MEM_WIDTH:]


def _memkv(mem2d, norm_mem, w_mem_kv):
    depth, d = norm_mem.shape
    m = mem2d.shape[0]
    out = jax.ShapeDtypeStruct((depth, m, MEM_WIDTH), F32)
    ospec = pl.BlockSpec((None, m, MEM_WIDTH), lambda l: (l, 0, 0))
    return pl.pallas_call(
        _memkv_body, grid=(depth,),
        in_specs=[pl.BlockSpec((m, d), lambda l: (0, 0)),
                  pl.BlockSpec((None, 1, d), lambda l: (l, 0, 0)),
                  pl.BlockSpec((None, d, 2 * MEM_WIDTH), lambda l: (l, 0, 0))],
        out_specs=[ospec, ospec], out_shape=[out, out],
        compiler_params=_cparams("parallel"), name="mem_kv",
    )(mem2d, norm_mem.reshape(depth, 1, d), w_mem_kv)


def _neg_upper(tk):
    return jnp.where(_iota((tk, tk), 0) > _iota((tk, tk), 1), -1.0, 0.0).astype(BF16)


def _sb_chunk(qh, ks, vs, uneg, carry, acc, mask):
    z = _dot_nt(qh, ks)
    sp = jnp.maximum(z, 0.0) + jnp.log(1.0 + jnp.exp(-jnp.abs(z)))
    spm = sp if mask is None else jnp.where(mask, sp, 0.0)
    hi = spm.astype(BF16)
    lo = (spm - hi.astype(F32)).astype(BF16)
    between = _dot(hi, uneg) + _dot(lo, uneg) + carry
    a = jnp.exp(z - sp + between)
    if mask is not None:
        a = jnp.where(mask, a, 0.0)
    acc = acc + _dot(a.astype(BF16), vs)
    carry = carry - jnp.sum(spm, axis=-1, keepdims=True)
    return carry, acc


def _sm_chunk(qh, ks, vs, m, l, acc, mask):
    s = _dot_nt(qh, ks)
    if mask is not None:
        s = jnp.where(mask, s, NEG_BIG)
    m_new = jnp.maximum(m, jnp.max(s, axis=-1, keepdims=True))
    alpha = jnp.exp2(m - m_new)
    p = jnp.exp2(s - jnp.concatenate([m_new] * (s.shape[1] // LANES), axis=1))
    l = alpha * l + jnp.sum(p, axis=-1, keepdims=True)
    acc = alpha * acc + _dot(p.astype(BF16), vs)
    return m_new, l, acc


def _diff_lambda(lam_ref, lam_init):
    lv = lam_ref[...]
    s1 = jnp.sum(lv[0:1] * lv[1:2], axis=-1, keepdims=True)
    s2 = jnp.sum(lv[2:3] * lv[3:4], axis=-1, keepdims=True)
    return jnp.exp(s1) - jnp.exp(s2) + lam_init


def _head_rmsnorm(o, gain, lam_init):
    return _rmsnorm_rows(o, gain) * (1.0 - lam_init)


def _mem_attention(qbd, mk, mv):
    s = _dot_nt(qbd, mk)
    p = jnp.exp(s - jnp.max(s, axis=-1, keepdims=True))
    o = _dot(p.astype(BF16), mv)
    return o / jnp.sum(p, axis=-1, keepdims=True)


def _sb_prompt_body(q_ref, k_ref, v_ref, o_ref, acc_ref, carry_ref, *, t):
    i = pl.program_id(2)
    q = q_ref[...].astype(F32)
    lane = _iota((t, LANES), 1)
    qhs = (jnp.where(lane < HEAD_DIM, q, 0.0).astype(BF16), jnp.where(lane >= HEAD_DIM, q, 0.0).astype(BF16))
    uneg = _neg_upper(t)
    causal = _iota((t, t), 1) < _iota((t, t), 0)

    def process(c, mask):
        start = pl.multiple_of(c * t, t)
        ks = k_ref[pl.ds(start, t), :]
        vs = v_ref[pl.ds(start, t), :]
        for h in range(2):
            carry, acc = _sb_chunk(qhs[h], ks, vs, uneg, carry_ref[h], acc_ref[h], mask)
            carry_ref[h] = carry
            acc_ref[h] = acc

    def alive():
        return (jnp.max(carry_ref[...]) > SB_DEAD).astype(jnp.int32)

    carry_ref[...] = jnp.zeros_like(carry_ref)
    acc_ref[...] = jnp.zeros_like(acc_ref)

    @pl.when(i == 0)
    def _():
        process(i, causal)

    @pl.when(i > 0)
    def _():
        process(i, causal)
        process(i - 1, None)

    def body(st):
        process(st[0], None)
        return st[0] - 1, alive()

    lax.while_loop(lambda st: (st[0] >= 0) & (st[1] > 0), body, (i - 2, alive()))
    o_ref[...] = jnp.where(lane < HEAD_DIM, acc_ref[0], acc_ref[1]).astype(BF16)


def _diff_prompt_body(lam_ref, gain_ref, q_ref, k_ref, v_ref, o_ref, m_ref, l_ref, acc_ref, *, t, lam_init):
    i = pl.program_id(2)
    q = q_ref[...].astype(F32)
    lane = _iota((t, LANES), 1)
    qhs = (jnp.where(lane < HEAD_DIM, q, 0.0).astype(BF16), jnp.where(lane >= HEAD_DIM, q, 0.0).astype(BF16))

    def process(c, mask):
        start = pl.multiple_of(c * t, t)
        ks = k_ref[pl.ds(start, t), :]
        vs = v_ref[pl.ds(start, t), :]
        for h in range(2):
            m, l, acc = _sm_chunk(qhs[h], ks, vs, m_ref[h], l_ref[h], acc_ref[h], mask)
            m_ref[h] = m
            l_ref[h] = l
            acc_ref[h] = acc

    m_ref[...] = jnp.full_like(m_ref, NEG_BIG)
    l_ref[...] = jnp.zeros_like(l_ref)
    acc_ref[...] = jnp.zeros_like(acc_ref)

    def body(c, carry):
        process(c, None)
        return carry

    lax.fori_loop(0, i, body, 0)
    process(i, _iota((t, t), 1) <= _iota((t, t), 0))
    lam = _diff_lambda(lam_ref, lam_init)
    o = acc_ref[0] / l_ref[0] - lam * (acc_ref[1] / l_ref[1])
    o_ref[...] = _head_rmsnorm(o, gain_ref[...], lam_init).astype(BF16)


def _prompt_attention(qb, kb, vb, batch, seq, diff_params):
    n_rows = batch * seq
    t = min(ATTN_TILE if diff_params is None else DIFF_TILE, seq)
    nq = seq // t
    nh = MIX_WIDTH // LANES
    qspec = pl.BlockSpec((t, LANES), lambda b, h, i: (b * nq + i, h))
    kvspec = pl.BlockSpec((seq, LANES), lambda b, h, i: (b, h))
    out_shape = jax.ShapeDtypeStruct((n_rows, MIX_WIDTH), BF16)
    if diff_params is None:
        body = functools.partial(_sb_prompt_body, t=t)
        in_specs, args = [qspec, kvspec, kvspec], [qb, kb, vb]
        scratch = [pltpu.VMEM((2, t, LANES), F32), pltpu.VMEM((2, t, 1), F32)]
        name = "sb_prompt"
    else:
        lam_vecs, gain, lam_init = diff_params
        body = functools.partial(_diff_prompt_body, t=t, lam_init=lam_init)
        in_specs = [pl.BlockSpec((4, HEAD_DIM), lambda b, h, i: (0, 0)),
                    pl.BlockSpec((1, LANES), lambda b, h, i: (0, 0)), qspec, kvspec, kvspec]
        args = [lam_vecs, gain.reshape(1, LANES), qb, kb, vb]
        scratch = [pltpu.VMEM((2, t, LANES), F32)] * 3
        name = "diff_prompt"
    return pl.pallas_call(
        body, grid=(batch, nh, nq), in_specs=in_specs, out_specs=qspec, out_shape=out_shape,
        scratch_shapes=scratch, compiler_params=_cparams("parallel", "parallel", "arbitrary"), name=name,
    )(*args)


def _mem_prompt_body(q_ref, mk_ref, mv_ref, o_ref):
    q = q_ref[...].astype(F32)
    mk = mk_ref[...].astype(BF16)
    mv = mv_ref[...].astype(BF16)
    lane_head = _iota(q.shape, 1) // HEAD_DIM
    out = jnp.zeros(q.shape, F32)
    for h in range(MEM_HEADS):
        o = _mem_attention(jnp.where(lane_head == h, q, 0.0).astype(BF16), mk, mv)
        out = jnp.where(lane_head == h, o, out)
    o_ref[...] = out.astype(BF16)


def _mem_prompt(qm, mk, mv, batch, seq):
    n_rows = batch * seq
    t = min(ROW_TILE, seq)
    nq = seq // t
    n_mem = mk.shape[1]
    qspec = pl.BlockSpec((t, MEM_WIDTH), lambda b, i: (b * nq + i, 0))
    mspec = pl.BlockSpec((None, n_mem, MEM_WIDTH), lambda b, i: (b, 0, 0))
    return pl.pallas_call(
        _mem_prompt_body, grid=(batch, nq), in_specs=[qspec, mspec, mspec], out_specs=qspec,
        out_shape=jax.ShapeDtypeStruct((n_rows, MEM_WIDTH), BF16),
        compiler_params=_cparams("parallel", "parallel"), name="mem_prompt",
    )(qm, mk, mv)


def _block_diag_queries(q, n_blocks, block_lanes):
    dt = q.shape[0]
    qt = jnp.concatenate([q] * n_blocks, axis=0)
    keep = (_iota(qt.shape, 0) // dt) == (_iota(qt.shape, 1) // block_lanes)
    return jnp.where(keep, qt, 0.0).astype(BF16)


def _fold_blocks(full, dt, block_lanes):
    keep = (_iota(full.shape, 0) // dt) == (_iota(full.shape, 1) // block_lanes)
    sel = jnp.where(keep, full, 0.0)
    out = sel[0:dt]
    for j in range(1, full.shape[0] // dt):
        out = out + sel[j * dt:(j + 1) * dt]
    return out


def _sample_body(pt_ref, *refs, n_pages, page, dt, diff, lam_init):
    del pt_ref
    if diff:
        lam_ref, gain_ref = refs[:2]
        refs = refs[2:]
    q_ref, kn_ref, vn_ref, qm_ref, mk_ref, mv_ref = refs[:6]
    k_pages = refs[6:6 + n_pages]
    v_pages = refs[6 + n_pages:6 + 2 * n_pages]
    o_ref, om_ref, kall, vall = refs[6 + 2 * n_pages:]

    n_maps = MIX_WIDTH // HEAD_DIM
    rows = n_maps * dt
    past = n_pages * page
    qbd = _block_diag_queries(q_ref[...].astype(F32), n_maps, HEAD_DIM)
    for p in range(n_pages):
        kall[p * page:(p + 1) * page, :] = k_pages[p][...].astype(BF16)
        vall[p * page:(p + 1) * page, :] = v_pages[p][...].astype(BF16)
    pad = jnp.zeros((page - dt, MIX_WIDTH), F32)
    kall[past:, :] = jnp.concatenate([kn_ref[...].astype(F32), pad], axis=0).astype(BF16)
    vall[past:, :] = jnp.concatenate([vn_ref[...].astype(F32), pad], axis=0).astype(BF16)
    tpos = _iota((rows, page), 0) % dt
    col = _iota((rows, page), 1)
    z = _dot_nt(qbd, kall[...])
    chunks = [z[:, c * page:(c + 1) * page] for c in range(n_pages + 1)]

    if not diff:
        new_ok = col < tpos
        sp = [jnp.maximum(zc, 0.0) + jnp.log(1.0 + jnp.exp(-jnp.abs(zc))) for zc in chunks]
        spm = sp[:-1] + [jnp.where(new_ok, sp[-1], 0.0)]
        stacked = jnp.concatenate(spm, axis=0)
        hi = stacked.astype(BF16)
        lo = (stacked - hi.astype(F32)).astype(BF16)
        uneg = _neg_upper(page)
        later = _dot(hi, uneg) + _dot(lo, uneg)
        carry = jnp.zeros((rows, 1), F32)
        a = [None] * (n_pages + 1)
        for c in reversed(range(n_pages + 1)):
            a[c] = jnp.exp(chunks[c] - sp[c] + later[c * rows:(c + 1) * rows] + carry)
            carry = carry - jnp.sum(spm[c], axis=-1, keepdims=True)
        a[-1] = jnp.where(new_ok, a[-1], 0.0)
        acc = _dot(jnp.concatenate(a, axis=1).astype(BF16), vall[...])
        o = _fold_blocks(acc, dt, HEAD_DIM)
    else:
        s = jnp.concatenate(chunks[:-1] + [jnp.where(col <= tpos, chunks[-1], NEG_BIG)], axis=1)
        p = jnp.exp2(s - jnp.max(s, axis=-1, keepdims=True))
        l = jnp.sum(p, axis=-1, keepdims=True)
        acc = _dot(p.astype(BF16), vall[...])
        lam = _diff_lambda(lam_ref, lam_init)
        second_map = ((_iota((rows, 1), 0) // dt) % 2) == 1
        weight = jnp.where(second_map, -lam, 1.0) / l
        o = _fold_blocks(acc * weight, 2 * dt, 2 * HEAD_DIM)
        o = o[0:dt] + o[dt:2 * dt]
        gain = gain_ref[...]
        o = jnp.concatenate(
            [_head_rmsnorm(o[:, j * LANES:(j + 1) * LANES], gain, lam_init) for j in range(DIFF_HEADS)], axis=1)
    o_ref[...] = o.astype(BF16)

    qmbd = _block_diag_queries(qm_ref[...].astype(F32), MEM_HEADS, HEAD_DIM)
    om = _mem_attention(qmbd, mk_ref[...].astype(BF16), mv_ref[...].astype(BF16))
    om_ref[...] = _fold_blocks(om, dt, HEAD_DIM).astype(BF16)


def _sample_attention(layer, q3, kn3, vn3, qm3, cache_k, cache_v, cache_mem_k, cache_mem_v, page_table,
                      diff_params):
    db, dt, _ = q3.shape
    n_pages = page_table.shape[1]
    page = cache_k.shape[2]
    n_mem = cache_mem_k.shape[2]
    diff = diff_params is not None

    def req(w):
        return pl.BlockSpec((None, dt, w), lambda b, pt: (b, 0, 0))

    def page_spec(p):
        return pl.BlockSpec((None, None, page, MIX_WIDTH), lambda b, pt: (layer, pt[b * n_pages + p], 0, 0))

    mem_spec = pl.BlockSpec((None, None, n_mem, MEM_WIDTH), lambda b, pt: (layer, b, 0, 0))
    in_specs = [req(MIX_WIDTH)] * 3 + [req(MEM_WIDTH), mem_spec, mem_spec]
    in_specs += [page_spec(p) for p in range(n_pages)] * 2
    args = [q3, kn3, vn3, qm3, cache_mem_k, cache_mem_v] + [cache_k] * n_pages + [cache_v] * n_pages
    lam_init = 0.0
    if diff:
        lam_vecs, gain, lam_init = diff_params
        in_specs = [pl.BlockSpec((4, HEAD_DIM), lambda b, pt: (0, 0)),
                    pl.BlockSpec((1, LANES), lambda b, pt: (0, 0))] + in_specs
        args = [lam_vecs, gain.reshape(1, LANES)] + args
    body = functools.partial(_sample_body, n_pages=n_pages, page=page, dt=dt, diff=diff, lam_init=lam_init)
    return pl.pallas_call(
        body,
        grid_spec=pltpu.PrefetchScalarGridSpec(
            num_scalar_prefetch=1, grid=(db,), in_specs=in_specs,
            out_specs=[req(MIX_WIDTH), req(MEM_WIDTH)],
            scratch_shapes=[pltpu.VMEM(((n_pages + 1) * page, MIX_WIDTH), BF16)] * 2),
        out_shape=[jax.ShapeDtypeStruct((db, dt, MIX_WIDTH), BF16),
                   jax.ShapeDtypeStruct((db, dt, MEM_WIDTH), BF16)],
        compiler_params=_cparams("arbitrary"), name="diff_sample" if diff else "sb_sample",
    )(page_table.reshape(-1), *args)


def _out_router_body(x_ref, op_ref, omp_ref, os_ref, oms_ref, wo_ref, g_ref, wrh_ref, wrl_ref, br_ref,
                     y_ref, xn_ref, info_ref, cnt_ref, run_ref, *, prompt_tiles):
    @pl.when(pl.program_id(0) == 0)
    def _():
        run_ref[...] = jnp.zeros_like(run_ref)

    is_prompt = pl.program_id(0) < prompt_tiles
    o = jnp.where(is_prompt, op_ref[...], os_ref[...])
    om = jnp.where(is_prompt, omp_ref[...], oms_ref[...])
    y = x_ref[...] + _dot(o, wo_ref[0:MIX_WIDTH, :]) + _dot(om, wo_ref[MIX_WIDTH:, :])
    y_ref[...] = y
    xn = _rmsnorm_rows(y, g_ref[...])
    xn_ref[...] = xn
    xh = xn.astype(BF16)
    xl = (xn - xh.astype(F32)).astype(BF16)
    wh, wl = wrh_ref[...], wrl_ref[...]
    logit = _dot(xh, wh) + _dot(xl, wh) + _dot(xh, wl) + br_ref[...]

    tm = logit.shape[0]
    lane = _iota((tm, LANES), 1)
    lg = jnp.where(lane < N_GROUPS, logit, -jnp.inf)
    mg = jnp.max(lg, axis=-1, keepdims=True)
    g_idx = jnp.min(jnp.where(lg == mg, lane, LANES), axis=-1, keepdims=True)
    g_w = 1.0 / jnp.sum(jnp.exp(lg - mg), axis=-1, keepdims=True)
    in_group = (lane >= ROUTE_LANE0) & (((lane - ROUTE_LANE0) >> 3) == g_idx) & (lane < ROUTE_LANE0 + N_EXPERTS)
    le = jnp.where(in_group, logit, -jnp.inf)
    v1 = jnp.max(le, axis=-1, keepdims=True)
    i1 = jnp.min(jnp.where(le == v1, lane, LANES), axis=-1, keepdims=True)
    le2 = jnp.where(lane == i1, -jnp.inf, le)
    v2 = jnp.max(le2, axis=-1, keepdims=True)
    i2 = jnp.min(jnp.where(le2 == v2, lane, LANES), axis=-1, keepdims=True)
    e21 = jnp.exp(v2 - v1)
    w1 = 1.0 / (1.0 + e21)
    w2 = e21 * w1

    hit1, hit2 = lane == i1, lane == i2
    onehot = jnp.where(hit1 | hit2, 1.0, 0.0)
    lower = jnp.where(_iota((tm, tm), 1) < _iota((tm, tm), 0), 1.0, 0.0).astype(BF16)
    before = _dot(lower, onehot.astype(BF16)) + run_ref[...]
    r1 = jnp.sum(jnp.where(hit1, before, 0.0), axis=-1, keepdims=True)
    r2 = jnp.sum(jnp.where(hit2, before, 0.0), axis=-1, keepdims=True)
    run_ref[...] = run_ref[...] + jnp.sum(onehot, axis=0, keepdims=True)
    cnt_ref[...] = run_ref[...]

    info = jnp.zeros((tm, LANES), F32)
    fields = ((i1 - ROUTE_LANE0).astype(F32), (i2 - ROUTE_LANE0).astype(F32), g_w * w1, g_w * w2, r1, r2)
    for j, val in enumerate(fields):
        info = jnp.where(lane == j, val, info)
    info_ref[...] = info


def _out_router(x, o_p, om_p, o_s, om_s, wo_bf16, g_ffn, wr_hi, wr_lo, b_route):
    n, d = x.shape
    tm = ROW_TILE
    pt = o_p.shape[0] // tm
    row = lambda w: pl.BlockSpec((tm, w), lambda i: (i, 0))
    prow = lambda w: pl.BlockSpec((tm, w), lambda i: (jnp.minimum(i, pt - 1), 0))
    srow = lambda w: pl.BlockSpec((tm, w), lambda i: (jnp.maximum(i - pt, 0), 0))
    full = lambda a: pl.BlockSpec(a.shape, lambda i: (0, 0))
    g2 = g_ffn.reshape(1, d)
    return pl.pallas_call(
        functools.partial(_out_router_body, prompt_tiles=pt), grid=(n // tm,),
        in_specs=[row(d), prow(MIX_WIDTH), prow(MEM_WIDTH), srow(MIX_WIDTH), srow(MEM_WIDTH),
                  full(wo_bf16), full(g2), full(wr_hi), full(wr_lo), full(b_route)],
        out_specs=[row(d), row(d), row(LANES), pl.BlockSpec((1, LANES), lambda i: (0, 0))],
        out_shape=[jax.ShapeDtypeStruct((n, d), F32), jax.ShapeDtypeStruct((n, d), F32),
                   jax.ShapeDtypeStruct((n, LANES), F32), jax.ShapeDtypeStruct((1, LANES), F32)],
        scratch_shapes=[pltpu.VMEM((1, LANES), F32)],
        compiler_params=_cparams("arbitrary"), name="out_router",
    )(x, o_p, om_p, o_s, om_s, wo_bf16, g2, wr_hi, wr_lo, b_route)


def _expert_body(be_ref, nused_ref, x_ref, wgu_ref, wdn_ref, y_ref, wgu_s, wdn_s):
    i = pl.program_id(0)
    prev = be_ref[jnp.maximum(i - 1, 0)]

    @pl.when((i == 0) | (be_ref[i] != prev))
    def _():
        wgu_s[...] = wgu_ref[...].astype(BF16)
        wdn_s[...] = wdn_ref[...].astype(BF16)

    @pl.when(i < nused_ref[0])
    def _():
        h = _dot(x_ref[...].astype(BF16), wgu_s[...])
        a, u = h[:, :EXPERT_FF], h[:, EXPERT_FF:]
        act = a * (1.0 / (1.0 + jnp.exp(-a))) * u
        y_ref[...] = _dot(act.astype(BF16), wdn_s[...])

    @pl.when(i >= nused_ref[0])
    def _():
        y_ref[...] = jnp.zeros_like(y_ref)


def _experts(layer, xs, block_expert, n_used, w_gu, w_dn):
    rows, d = xs.shape
    nb = rows // MOE_ROWS
    return pl.pallas_call(
        _expert_body,
        grid_spec=pltpu.PrefetchScalarGridSpec(
            num_scalar_prefetch=2, grid=(nb,),
            in_specs=[pl.BlockSpec((MOE_ROWS, d), lambda i, be, nu: (i, 0)),
                      pl.BlockSpec((None, None, d, 2 * EXPERT_FF), lambda i, be, nu: (layer, be[i], 0, 0)),
                      pl.BlockSpec((None, None, EXPERT_FF, d), lambda i, be, nu: (layer, be[i], 0, 0))],
            out_specs=pl.BlockSpec((MOE_ROWS, d), lambda i, be, nu: (i, 0)),
            scratch_shapes=[pltpu.VMEM((d, 2 * EXPERT_FF), BF16), pltpu.VMEM((EXPERT_FF, d), BF16)]),
        out_shape=jax.ShapeDtypeStruct((rows, d), F32),
        compiler_params=_cparams("arbitrary"), name="experts",
    )(block_expert, n_used, xs, w_gu, w_dn)


def _move_rows_body(start_ref, e_hbm, r_hbm, src_hbm, *rest, to_sorted):
    dst_hbm, e_s, r_s, idx_sem, row_sem = rest[-5:]
    i = pl.program_id(0)
    base = pl.multiple_of(i * MOVE_ROWS, MOVE_ROWS)
    copy_e = pltpu.make_async_copy(e_hbm.at[pl.ds(base, MOVE_ROWS)], e_s, idx_sem.at[0])
    copy_r = pltpu.make_async_copy(r_hbm.at[pl.ds(base, MOVE_ROWS)], r_s, idx_sem.at[1])
    copy_e.start()
    copy_r.start()
    copy_e.wait()
    copy_r.wait()

    def issue(j, carry):
        sorted_row = start_ref[e_s[j]] + r_s[j]
        a = base + j
        if to_sorted:
            src, dst = src_hbm.at[pl.ds(a >> 1, 1)], dst_hbm.at[pl.ds(sorted_row, 1)]
        else:
            src, dst = src_hbm.at[pl.ds(sorted_row, 1)], dst_hbm.at[pl.ds(a, 1)]
        pltpu.make_async_copy(src, dst, row_sem).start()
        return carry

    lax.fori_loop(0, MOVE_ROWS, issue, 0, unroll=8)

    def drain_one_batch():
        pltpu.make_async_copy(src_hbm.at[pl.ds(0, MOVE_ROWS)], dst_hbm.at[pl.ds(0, MOVE_ROWS)], row_sem).wait()

    @pl.when(i > 0)
    def _():
        drain_one_batch()

    @pl.when(i == pl.num_programs(0) - 1)
    def _():
        drain_one_batch()


def _move_rows(start, e_flat, r_flat, src, dst_rows, dst_init, to_sorted):
    n_assign = e_flat.shape[0]
    d = src.shape[1]
    any_spec = pl.BlockSpec(memory_space=pl.ANY)
    args = [start, e_flat, r_flat, src]
    aliases = {}
    if dst_init is not None:
        aliases = {len(args): 0}
        args.append(dst_init)
    return pl.pallas_call(
        functools.partial(_move_rows_body, to_sorted=to_sorted),
        grid_spec=pltpu.PrefetchScalarGridSpec(
            num_scalar_prefetch=1, grid=(n_assign // MOVE_ROWS,),
            in_specs=[any_spec] * (len(args) - 1), out_specs=any_spec,
            scratch_shapes=[pltpu.SMEM((MOVE_ROWS,), jnp.int32), pltpu.SMEM((MOVE_ROWS,), jnp.int32),
                            pltpu.SemaphoreType.DMA((2,)), pltpu.SemaphoreType.DMA(())]),
        out_shape=jax.ShapeDtypeStruct((dst_rows, d), F32),
        input_output_aliases=aliases,
        compiler_params=_cparams("arbitrary"), name="to_sorted" if to_sorted else "from_sorted",
    )(*args)


def _moe(layer, y, xn, info, counts, xs_buf, w_gu, w_dn):
    n, d = y.shape
    e_flat = info[:, 0:2].astype(jnp.int32).reshape(-1)
    r_flat = info[:, 4:6].astype(jnp.int32).reshape(-1)
    gate = info[:, 2:4]
    cnt = counts[0, ROUTE_LANE0:ROUTE_LANE0 + N_EXPERTS].astype(jnp.int32)
    padded = (cnt + MOE_ROWS - 1) // MOE_ROWS * MOE_ROWS
    pad_end = jnp.cumsum(padded)
    pad_start = pad_end - padded
    nb = xs_buf.shape[0] // MOE_ROWS
    block_row0 = jnp.arange(nb, dtype=jnp.int32) * MOE_ROWS
    block_expert = jnp.minimum(jnp.sum((pad_end[None, :] <= block_row0[:, None]).astype(jnp.int32), axis=1),
                               N_EXPERTS - 1)
    n_used = (pad_end[-1:] // MOE_ROWS).astype(jnp.int32)
    xs = _move_rows(pad_start, e_flat, r_flat, xn, xs_buf.shape[0], xs_buf, True)
    ys = _experts(layer, xs, block_expert, n_used, w_gu, w_dn)
    y2 = _move_rows(pad_start, e_flat, r_flat, ys, 2 * n, None, False).reshape(n, 2, d)
    return y + gate[:, 0:1] * y2[:, 0] + gate[:, 1:2] * y2[:, 1], xs


def _final_norm_body(x_ref, g_ref, o_ref):
    o_ref[...] = _rmsnorm_rows(x_ref[...], g_ref[...])


def _final_norm(x, row0, n_rows, g):
    d = x.shape[1]
    tm = ROW_TILE
    t0 = row0 // tm
    return pl.pallas_call(
        _final_norm_body, grid=(n_rows // tm,),
        in_specs=[pl.BlockSpec((tm, d), lambda i: (t0 + i, 0)), pl.BlockSpec((1, d), lambda i: (0, 0))],
        out_specs=pl.BlockSpec((tm, d), lambda i: (i, 0)), out_shape=jax.ShapeDtypeStruct((n_rows, d), F32),
        compiler_params=_cparams("parallel"), name="final_norm",
    )(x, g.reshape(1, d))


def kernel(x_prompt, x_sample, mem_prompt, cache_k, cache_v, cache_mem_k, cache_mem_v, page_table, norm_attn, norm_mem, norm_ffn, norm_final, w_in, w_mem_kv, w_out, diff_lambda, diff_subln, w_router_group, b_router_group, w_router_expert, b_router_expert, w_expert_gu, w_expert_down):
    batch, seq, d = x_prompt.shape
    db, dt, _ = x_sample.shape
    depth = w_in.shape[0]
    n_mem = mem_prompt.shape[1]
    past_len = page_table.shape[1] * cache_k.shape[2]
    n_p, n_s = batch * seq, db * dt
    n = n_p + n_s
    assert seq % ROW_TILE == 0 and n_s % ROW_TILE == 0 and ROW_TILE % dt == 0 and seq % DIFF_TILE == 0

    x = jnp.concatenate([x_prompt.reshape(n_p, d), x_sample.reshape(n_s, d)], axis=0)
    seq_tiles = seq // ROW_TILE
    pos = jnp.concatenate([jnp.arange(seq, dtype=jnp.int32),
                           past_len + jnp.arange(ROW_TILE, dtype=jnp.int32) % dt])
    rot_tables = _rotary_tables(pos)

    mem_k, mem_v = _memkv(mem_prompt.reshape(batch * n_mem, d), norm_mem, w_mem_kv)
    mem_k = mem_k.reshape(depth, batch, n_mem, MEM_WIDTH)
    mem_v = mem_v.reshape(depth, batch, n_mem, MEM_WIDTH)

    sorted_rows = -(-(2 * n + N_EXPERTS * (MOE_ROWS - 1)) // MOE_ROWS) * MOE_ROWS
    xs_buf = jnp.zeros((sorted_rows, d), F32)
    kv_p = [jnp.zeros((depth, n_p, MIX_WIDTH), F32)] * 2
    kv_s = [jnp.zeros((depth, n_s, MIX_WIDTH), F32)] * 2
    for i in range(depth):
        is_diff = i % 2 == 1
        diff_params = None
        q_scale = QK_SCALE
        if is_diff:
            lam_init = 0.8 - 0.6 * math.exp(-0.3 * i)
            diff_params = (diff_lambda[i // 2], diff_subln[i // 2], lam_init)
            q_scale = QK_SCALE * LOG2E
        w_in_i = w_in[i].astype(BF16)
        rot_p = (rot_tables, 0, seq_tiles) if is_diff else None
        rot_s = (rot_tables, seq_tiles, 1) if is_diff else None
        *kv_p, qb, kb, vb, qm = _proj(x, 0, n_p, norm_attn[i], w_in_i, q_scale, rot_p, i, kv_p)
        *kv_s, qb_s, kb_s, vb_s, qm_s = _proj(x, n_p, n_s, norm_attn[i], w_in_i, q_scale, rot_s, i, kv_s)

        o = _prompt_attention(qb, kb, vb, batch, seq, diff_params)
        om = _mem_prompt(qm, mem_k[i], mem_v[i], batch, seq)
        s3 = lambda a: a.reshape(db, dt, a.shape[-1])
        o_s, om_s = _sample_attention(i, s3(qb_s), s3(kb_s), s3(vb_s), s3(qm_s), cache_k, cache_v,
                                      cache_mem_k, cache_mem_v, page_table, diff_params)
        o_s = o_s.reshape(n_s, MIX_WIDTH)
        om_s = om_s.reshape(n_s, MEM_WIDTH)

        w_route = jnp.zeros((d, LANES), F32)
        w_route = w_route.at[:, :N_GROUPS].set(w_router_group[i])
        w_route = w_route.at[:, ROUTE_LANE0:ROUTE_LANE0 + N_EXPERTS].set(w_router_expert[i])
        wr_hi = w_route.astype(BF16)
        wr_lo = (w_route - wr_hi.astype(F32)).astype(BF16)
        b_route = jnp.zeros((1, LANES), F32)
        b_route = b_route.at[0, :N_GROUPS].set(b_router_group[i])
        b_route = b_route.at[0, ROUTE_LANE0:ROUTE_LANE0 + N_EXPERTS].set(b_router_expert[i].reshape(-1))
        y, xn, info, counts = _out_router(x, o, om, o_s, om_s, w_out[i].astype(BF16), norm_ffn[i],
                                          wr_hi, wr_lo, b_route)
        x, xs_buf = _moe(i, y, xn, info, counts, xs_buf, w_expert_gu, w_expert_down)

    y_p = _final_norm(x, 0, n_p, norm_final).reshape(batch, seq, d)
    y_s = _final_norm(x, n_p, n_s, norm_final).reshape(db, dt, d)
    prompt = lambda a: a.reshape(depth, batch, seq, MIX_WIDTH)
    sample = lambda a: a.reshape(depth, db, dt, MIX_WIDTH)
    return (y_p, y_s, prompt(kv_p[0]), prompt(kv_p[1]), mem_k, mem_v, sample(kv_s[0]), sample(kv_s[1]))
```

```python
import functools
import math

import jax
import jax.numpy as jnp
from jax import lax
from jax.experimental import pallas as pl
from jax.experimental.pallas import tpu as pltpu

F32 = jnp.float32
BF16 = jnp.bfloat16

HEAD_DIM = 64
SB_HEADS = 12
DIFF_HEADS = 6
MEM_HEADS = 4
MIX_WIDTH = SB_HEADS * HEAD_DIM
MEM_WIDTH = MEM_HEADS * HEAD_DIM
ROT_DIM = HEAD_DIM // 4
ROPE_THETA = 500000.0
N_GROUPS = 8
EXPERTS_PER_GROUP = 8
N_EXPERTS = N_GROUPS * EXPERTS_PER_GROUP
EXPERT_FF = 256
RMS_EPS = 1e-6
QK_SCALE = HEAD_DIM ** -0.5

LANES = 128
ROW_TILE = 512
ATTN_TILE = 256
DIFF_TILE = 512
LOG2E = 1.4426950408889634
MOE_ROWS = 128
GATHER_SPLIT = 3
NEG_BIG = -1e30
SB_DEAD = -104.0
VMEM_LIMIT = 48 * 1024 * 1024
ROUTE_LANE0 = N_GROUPS


def _cparams(*sem):
    return pltpu.CompilerParams(dimension_semantics=sem, vmem_limit_bytes=VMEM_LIMIT)


def _rmsnorm_rows(x, g):
    ms = jnp.mean(x * x, axis=-1, keepdims=True)
    return x * lax.rsqrt(ms + RMS_EPS) * g


def _dot(a, b):
    return jnp.dot(a, b, preferred_element_type=F32)


def _dot_nt(a, b):
    return lax.dot_general(a, b, (((1,), (1,)), ((), ())), preferred_element_type=F32)


def _iota(shape, axis):
    return lax.broadcasted_iota(jnp.int32, shape, axis)


def _proj_body(rotary, q_scale, x_ref, g_ref, w_ref, *rest):
    if rotary:
        cos_ref, sa_ref, sb_ref = rest[:3]
        rest = rest[3:]
    kf_ref, vf_ref, qb_ref, kb_ref, vb_ref, qm_ref = rest[-6:]
    xn = _rmsnorm_rows(x_ref[...], g_ref[...]).astype(BF16)
    q = _dot(xn, w_ref[:, 0:MIX_WIDTH])
    k = _dot(xn, w_ref[:, MIX_WIDTH:2 * MIX_WIDTH])
    v = _dot(xn, w_ref[:, 2 * MIX_WIDTH:3 * MIX_WIDTH])
    qm = _dot(xn, w_ref[:, 3 * MIX_WIDTH:])
    if rotary:
        c, sa, sb = cos_ref[...], sa_ref[...], sb_ref[...]

        def rot(t):
            outs = []
            for j in range(MIX_WIDTH // LANES):
                s = t[:, j * LANES:(j + 1) * LANES]
                outs.append(s * c + pltpu.roll(s, LANES - ROT_DIM // 2, 1) * sa
                            + pltpu.roll(s, ROT_DIM // 2, 1) * sb)
            return jnp.concatenate(outs, axis=1)

        q = rot(q)
        k = rot(k)
    kf_ref[...] = k
    vf_ref[...] = v
    qb_ref[...] = (q * q_scale).astype(BF16)
    kb_ref[...] = k.astype(BF16)
    vb_ref[...] = v.astype(BF16)
    qm_ref[...] = (qm * QK_SCALE).astype(BF16)


def _proj(x, row0, n_rows, g, w_bf16, q_scale, rot, layer, kv_prev):
    d = x.shape[1]
    tm = ROW_TILE
    t0 = row0 // tm
    in_specs = [pl.BlockSpec((tm, d), lambda i: (t0 + i, 0)), pl.BlockSpec((1, d), lambda i: (0, 0)),
                pl.BlockSpec(w_bf16.shape, lambda i: (0, 0))]
    args = [x, g.reshape(1, d), w_bf16]
    if rot is not None:
        tables, first, nblk = rot
        in_specs += [pl.BlockSpec((tm, LANES), lambda i: (first + i % nblk, 0))] * 3
        args += list(tables)
    aliases = {len(args): 0, len(args) + 1: 1}
    in_specs += [pl.BlockSpec(memory_space=pl.ANY)] * 2
    args += list(kv_prev)
    row = lambda w: pl.BlockSpec((tm, w), lambda i: (i, 0))
    kv_spec = pl.BlockSpec((None, tm, MIX_WIDTH), lambda i: (layer, i, 0))
    out_shape = [jax.ShapeDtypeStruct(kv_prev[0].shape, F32)] * 2 + \
                [jax.ShapeDtypeStruct((n_rows, MIX_WIDTH), BF16)] * 3 + \
                [jax.ShapeDtypeStruct((n_rows, MEM_WIDTH), BF16)]
    out_specs = [kv_spec] * 2 + [row(MIX_WIDTH)] * 3 + [row(MEM_WIDTH)]
    return pl.pallas_call(
        functools.partial(_proj_body, rot is not None, q_scale),
        grid=(n_rows // tm,), in_specs=in_specs, out_specs=out_specs, out_shape=out_shape,
        input_output_aliases=aliases,
        compiler_params=_cparams("parallel"), name="proj_rot" if rot is not None else "proj",
    )(*args)


def _rotary_tables(pos):
    half = ROT_DIM // 2
    lane = jnp.arange(LANES) % HEAD_DIM
    inv_freq = 1.0 / (ROPE_THETA ** ((lane % half).astype(F32) * 2.0 / ROT_DIM))
    ang = pos.astype(F32)[:, None] * inv_freq[None, :]
    cos, sin = jnp.cos(ang), jnp.sin(ang)
    c = jnp.where(lane[None, :] < ROT_DIM, cos, 1.0)
    sa = jnp.where(lane[None, :] < half, -sin, 0.0)
    sb = jnp.where((lane[None, :] >= half) & (lane[None, :] < ROT_DIM), sin, 0.0)
    return c, sa, sb


def _memkv_body(mem_ref, g_ref, w_ref, k_ref, v_ref):
    xn = _rmsnorm_rows(mem_ref[...], g_ref[...]).astype(BF16)
    kv = _dot(xn, w_ref[...].astype(BF16))
    k_ref[...] = kv[:, :MEM_WIDTH]
    v_ref[...] = kv[:, MEM_WIDTH:]


def _memkv(mem2d, norm_mem, w_mem_kv):
    depth, d = norm_mem.shape
    m = mem2d.shape[0]
    out = jax.ShapeDtypeStruct((depth, m, MEM_WIDTH), F32)
    ospec = pl.BlockSpec((None, m, MEM_WIDTH), lambda l: (l, 0, 0))
    return pl.pallas_call(
        _memkv_body, grid=(depth,),
        in_specs=[pl.BlockSpec((m, d), lambda l: (0, 0)),
                  pl.BlockSpec((None, 1, d), lambda l: (l, 0, 0)),
                  pl.BlockSpec((None, d, 2 * MEM_WIDTH), lambda l: (l, 0, 0))],
        out_specs=[ospec, ospec], out_shape=[out, out],
        compiler_params=_cparams("parallel"), name="mem_kv",
    )(mem2d, norm_mem.reshape(depth, 1, d), w_mem_kv)


def _neg_upper(tk):
    return jnp.where(_iota((tk, tk), 0) > _iota((tk, tk), 1), -1.0, 0.0).astype(BF16)


def _sb_chunk(qh, ks, vs, uneg, carry, acc, mask):
    z = _dot_nt(qh, ks)
    sp = jnp.maximum(z, 0.0) + jnp.log(1.0 + jnp.exp(-jnp.abs(z)))
    spm = sp if mask is None else jnp.where(mask, sp, 0.0)
    hi = spm.astype(BF16)
    lo = (spm - hi.astype(F32)).astype(BF16)
    between = _dot(hi, uneg) + _dot(lo, uneg) + carry
    a = jnp.exp(z - sp + between)
    if mask is not None:
        a = jnp.where(mask, a, 0.0)
    acc = acc + _dot(a.astype(BF16), vs)
    carry = carry - jnp.sum(spm, axis=-1, keepdims=True)
    return carry, acc


def _sm_chunk(qh, ks, vs, m, l, acc, mask):
    s = _dot_nt(qh, ks)
    if mask is not None:
        s = jnp.where(mask, s, NEG_BIG)
    m_new = jnp.maximum(m, jnp.max(s, axis=-1, keepdims=True))
    alpha = jnp.exp2(m - m_new)
    p = jnp.exp2(s - jnp.concatenate([m_new] * (s.shape[1] // LANES), axis=1))
    l = alpha * l + jnp.sum(p, axis=-1, keepdims=True)
    acc = alpha * acc + _dot(p.astype(BF16), vs)
    return m_new, l, acc


def _diff_lambda(lam_ref, lam_init):
    lv = lam_ref[...]
    s1 = jnp.sum(lv[0:1] * lv[1:2], axis=-1, keepdims=True)
    s2 = jnp.sum(lv[2:3] * lv[3:4], axis=-1, keepdims=True)
    return jnp.exp(s1) - jnp.exp(s2) + lam_init


def _head_rmsnorm(o, gain, lam_init):
    return _rmsnorm_rows(o, gain) * (1.0 - lam_init)


def _mem_attention(qbd, mk, mv):
    s = _dot_nt(qbd, mk)
    p = jnp.exp(s - jnp.max(s, axis=-1, keepdims=True))
    o = _dot(p.astype(BF16), mv)
    return o / jnp.sum(p, axis=-1, keepdims=True)


def _sb_prompt_body(q_ref, k_ref, v_ref, o_ref, acc_ref, carry_ref, *, t):
    i = pl.program_id(2)
    q = q_ref[...].astype(F32)
    lane = _iota((t, LANES), 1)
    qhs = (jnp.where(lane < HEAD_DIM, q, 0.0).astype(BF16), jnp.where(lane >= HEAD_DIM, q, 0.0).astype(BF16))
    uneg = _neg_upper(t)
    causal = _iota((t, t), 1) < _iota((t, t), 0)

    def process(c, mask):
        start = pl.multiple_of(c * t, t)
        ks = k_ref[pl.ds(start, t), :]
        vs = v_ref[pl.ds(start, t), :]
        for h in range(2):
            carry, acc = _sb_chunk(qhs[h], ks, vs, uneg, carry_ref[h], acc_ref[h], mask)
            carry_ref[h] = carry
            acc_ref[h] = acc

    def alive():
        return (jnp.max(carry_ref[...]) > SB_DEAD).astype(jnp.int32)

    carry_ref[...] = jnp.zeros_like(carry_ref)
    acc_ref[...] = jnp.zeros_like(acc_ref)

    @pl.when(i == 0)
    def _():
        process(i, causal)

    @pl.when(i > 0)
    def _():
        process(i, causal)
        process(i - 1, None)

    def body(st):
        process(st[0], None)
        return st[0] - 1, alive()

    lax.while_loop(lambda st: (st[0] >= 0) & (st[1] > 0), body, (i - 2, alive()))
    o_ref[...] = jnp.where(lane < HEAD_DIM, acc_ref[0], acc_ref[1]).astype(BF16)


def _diff_prompt_body(lam_ref, gain_ref, q_ref, k_ref, v_ref, o_ref, m_ref, l_ref, acc_ref, *, t, lam_init):
    i = pl.program_id(2)
    q = q_ref[...].astype(F32)
    lane = _iota((t, LANES), 1)
    qhs = (jnp.where(lane < HEAD_DIM, q, 0.0).astype(BF16), jnp.where(lane >= HEAD_DIM, q, 0.0).astype(BF16))

    def process(c, mask):
        start = pl.multiple_of(c * t, t)
        ks = k_ref[pl.ds(start, t), :]
        vs = v_ref[pl.ds(start, t), :]
        for h in range(2):
            m, l, acc = _sm_chunk(qhs[h], ks, vs, m_ref[h], l_ref[h], acc_ref[h], mask)
            m_ref[h] = m
            l_ref[h] = l
            acc_ref[h] = acc

    m_ref[...] = jnp.full_like(m_ref, NEG_BIG)
    l_ref[...] = jnp.zeros_like(l_ref)
    acc_ref[...] = jnp.zeros_like(acc_ref)

    def body(c, carry):
        process(c, None)
        return carry

    lax.fori_loop(0, i, body, 0)
    process(i, _iota((t, t), 1) <= _iota((t, t), 0))
    lam = _diff_lambda(lam_ref, lam_init)
    o = acc_ref[0] / l_ref[0] - lam * (acc_ref[1] / l_ref[1])
    o_ref[...] = _head_rmsnorm(o, gain_ref[...], lam_init).astype(BF16)


def _prompt_attention(qb, kb, vb, batch, seq, diff_params):
    n_rows = batch * seq
    t = min(ATTN_TILE if diff_params is None else DIFF_TILE, seq)
    nq = seq // t
    nh = MIX_WIDTH // LANES
    qspec = pl.BlockSpec((t, LANES), lambda b, h, i: (b * nq + i, h))
    kvspec = pl.BlockSpec((seq, LANES), lambda b, h, i: (b, h))
    out_shape = jax.ShapeDtypeStruct((n_rows, MIX_WIDTH), BF16)
    if diff_params is None:
        body = functools.partial(_sb_prompt_body, t=t)
        in_specs, args = [qspec, kvspec, kvspec], [qb, kb, vb]
        scratch = [pltpu.VMEM((2, t, LANES), F32), pltpu.VMEM((2, t, 1), F32)]
        name = "sb_prompt"
    else:
        lam_vecs, gain, lam_init = diff_params
        body = functools.partial(_diff_prompt_body, t=t, lam_init=lam_init)
        in_specs = [pl.BlockSpec((4, HEAD_DIM), lambda b, h, i: (0, 0)),
                    pl.BlockSpec((1, LANES), lambda b, h, i: (0, 0)), qspec, kvspec, kvspec]
        args = [lam_vecs, gain.reshape(1, LANES), qb, kb, vb]
        scratch = [pltpu.VMEM((2, t, LANES), F32)] * 3
        name = "diff_prompt"
    return pl.pallas_call(
        body, grid=(batch, nh, nq), in_specs=in_specs, out_specs=qspec, out_shape=out_shape,
        scratch_shapes=scratch, compiler_params=_cparams("parallel", "parallel", "arbitrary"), name=name,
    )(*args)


def _mem_prompt_body(q_ref, mk_ref, mv_ref, o_ref):
    q = q_ref[...].astype(F32)
    mk = mk_ref[...].astype(BF16)
    mv = mv_ref[...].astype(BF16)
    lane_head = _iota(q.shape, 1) // HEAD_DIM
    out = jnp.zeros(q.shape, F32)
    for h in range(MEM_HEADS):
        o = _mem_attention(jnp.where(lane_head == h, q, 0.0).astype(BF16), mk, mv)
        out = jnp.where(lane_head == h, o, out)
    o_ref[...] = out.astype(BF16)


def _mem_prompt(qm, mk, mv, batch, seq):
    n_rows = batch * seq
    t = min(ROW_TILE, seq)
    nq = seq // t
    n_mem = mk.shape[1]
    qspec = pl.BlockSpec((t, MEM_WIDTH), lambda b, i: (b * nq + i, 0))
    mspec = pl.BlockSpec((None, n_mem, MEM_WIDTH), lambda b, i: (b, 0, 0))
    return pl.pallas_call(
        _mem_prompt_body, grid=(batch, nq), in_specs=[qspec, mspec, mspec], out_specs=qspec,
        out_shape=jax.ShapeDtypeStruct((n_rows, MEM_WIDTH), BF16),
        compiler_params=_cparams("parallel", "parallel"), name="mem_prompt",
    )(qm, mk, mv)


def _block_diag_queries(q, n_blocks, block_lanes):
    dt = q.shape[0]
    qt = jnp.concatenate([q] * n_blocks, axis=0)
    keep = (_iota(qt.shape, 0) // dt) == (_iota(qt.shape, 1) // block_lanes)
    return jnp.where(keep, qt, 0.0).astype(BF16)


def _fold_blocks(full, dt, block_lanes):
    keep = (_iota(full.shape, 0) // dt) == (_iota(full.shape, 1) // block_lanes)
    sel = jnp.where(keep, full, 0.0)
    out = sel[0:dt]
    for j in range(1, full.shape[0] // dt):
        out = out + sel[j * dt:(j + 1) * dt]
    return out


def _sample_body(pt_ref, *refs, n_pages, page, dt, diff, lam_init):
    del pt_ref
    if diff:
        lam_ref, gain_ref = refs[:2]
        refs = refs[2:]
    q_ref, kn_ref, vn_ref, qm_ref, mk_ref, mv_ref = refs[:6]
    k_pages = refs[6:6 + n_pages]
    v_pages = refs[6 + n_pages:6 + 2 * n_pages]
    o_ref, om_ref, kall, vall = refs[6 + 2 * n_pages:]

    n_maps = MIX_WIDTH // HEAD_DIM
    rows = n_maps * dt
    past = n_pages * page
    qbd = _block_diag_queries(q_ref[...].astype(F32), n_maps, HEAD_DIM)
    for p in range(n_pages):
        kall[p * page:(p + 1) * page, :] = k_pages[p][...].astype(BF16)
        vall[p * page:(p + 1) * page, :] = v_pages[p][...].astype(BF16)
    pad = jnp.zeros((page - dt, MIX_WIDTH), F32)
    kall[past:, :] = jnp.concatenate([kn_ref[...].astype(F32), pad], axis=0).astype(BF16)
    vall[past:, :] = jnp.concatenate([vn_ref[...].astype(F32), pad], axis=0).astype(BF16)
    tpos = _iota((rows, page), 0) % dt
    col = _iota((rows, page), 1)
    z = _dot_nt(qbd, kall[...])
    chunks = [z[:, c * page:(c + 1) * page] for c in range(n_pages + 1)]

    if not diff:
        new_ok = col < tpos
        sp = [jnp.maximum(zc, 0.0) + jnp.log(1.0 + jnp.exp(-jnp.abs(zc))) for zc in chunks]
        spm = sp[:-1] + [jnp.where(new_ok, sp[-1], 0.0)]
        stacked = jnp.concatenate(spm, axis=0)
        hi = stacked.astype(BF16)
        lo = (stacked - hi.astype(F32)).astype(BF16)
        uneg = _neg_upper(page)
        later = _dot(hi, uneg) + _dot(lo, uneg)
        carry = jnp.zeros((rows, 1), F32)
        a = [None] * (n_pages + 1)
        for c in reversed(range(n_pages + 1)):
            a[c] = jnp.exp(chunks[c] - sp[c] + later[c * rows:(c + 1) * rows] + carry)
            carry = carry - jnp.sum(spm[c], axis=-1, keepdims=True)
        a[-1] = jnp.where(new_ok, a[-1], 0.0)
        acc = _dot(jnp.concatenate(a, axis=1).astype(BF16), vall[...])
        o = _fold_blocks(acc, dt, HEAD_DIM)
    else:
        s = jnp.concatenate(chunks[:-1] + [jnp.where(col <= tpos, chunks[-1], NEG_BIG)], axis=1)
        p = jnp.exp2(s - jnp.max(s, axis=-1, keepdims=True))
        l = jnp.sum(p, axis=-1, keepdims=True)
        acc = _dot(p.astype(BF16), vall[...])
        lam = _diff_lambda(lam_ref, lam_init)
        second_map = ((_iota((rows, 1), 0) // dt) % 2) == 1
        weight = jnp.where(second_map, -lam, 1.0) / l
        o = _fold_blocks(acc * weight, 2 * dt, 2 * HEAD_DIM)
        o = o[0:dt] + o[dt:2 * dt]
        gain = gain_ref[...]
        o = jnp.concatenate(
            [_head_rmsnorm(o[:, j * LANES:(j + 1) * LANES], gain, lam_init) for j in range(DIFF_HEADS)], axis=1)
    o_ref[...] = o.astype(BF16)

    qmbd = _block_diag_queries(qm_ref[...].astype(F32), MEM_HEADS, HEAD_DIM)
    om = _mem_attention(qmbd, mk_ref[...].astype(BF16), mv_ref[...].astype(BF16))
    om_ref[...] = _fold_blocks(om, dt, HEAD_DIM).astype(BF16)


def _sample_attention(layer, q3, kn3, vn3, qm3, cache_k, cache_v, cache_mem_k, cache_mem_v, page_table,
                      diff_params):
    db, dt, _ = q3.shape
    n_pages = page_table.shape[1]
    page = cache_k.shape[2]
    n_mem = cache_mem_k.shape[2]
    diff = diff_params is not None

    def req(w):
        return pl.BlockSpec((None, dt, w), lambda b, pt: (b, 0, 0))

    def page_spec(p):
        return pl.BlockSpec((None, None, page, MIX_WIDTH), lambda b, pt: (layer, pt[b * n_pages + p], 0, 0))

    mem_spec = pl.BlockSpec((None, None, n_mem, MEM_WIDTH), lambda b, pt: (layer, b, 0, 0))
    in_specs = [req(MIX_WIDTH)] * 3 + [req(MEM_WIDTH), mem_spec, mem_spec]
    in_specs += [page_spec(p) for p in range(n_pages)] * 2
    args = [q3, kn3, vn3, qm3, cache_mem_k, cache_mem_v] + [cache_k] * n_pages + [cache_v] * n_pages
    lam_init = 0.0
    if diff:
        lam_vecs, gain, lam_init = diff_params
        in_specs = [pl.BlockSpec((4, HEAD_DIM), lambda b, pt: (0, 0)),
                    pl.BlockSpec((1, LANES), lambda b, pt: (0, 0))] + in_specs
        args = [lam_vecs, gain.reshape(1, LANES)] + args
    body = functools.partial(_sample_body, n_pages=n_pages, page=page, dt=dt, diff=diff, lam_init=lam_init)
    return pl.pallas_call(
        body,
        grid_spec=pltpu.PrefetchScalarGridSpec(
            num_scalar_prefetch=1, grid=(db,), in_specs=in_specs,
            out_specs=[req(MIX_WIDTH), req(MEM_WIDTH)],
            scratch_shapes=[pltpu.VMEM(((n_pages + 1) * page, MIX_WIDTH), BF16)] * 2),
        out_shape=[jax.ShapeDtypeStruct((db, dt, MIX_WIDTH), BF16),
                   jax.ShapeDtypeStruct((db, dt, MEM_WIDTH), BF16)],
        compiler_params=_cparams("arbitrary"), name="diff_sample" if diff else "sb_sample",
    )(page_table.reshape(-1), *args)


def _out_router_body(x_ref, op_ref, omp_ref, os_ref, oms_ref, wo_ref, g_ref, wrh_ref, wrl_ref, br_ref,
                     y_ref, xn_ref, info_ref, cnt_ref, run_ref, *, prompt_tiles):
    @pl.when(pl.program_id(0) == 0)
    def _():
        run_ref[...] = jnp.zeros_like(run_ref)

    is_prompt = pl.program_id(0) < prompt_tiles
    o = jnp.where(is_prompt, op_ref[...], os_ref[...])
    om = jnp.where(is_prompt, omp_ref[...], oms_ref[...])
    y = x_ref[...] + _dot(o, wo_ref[0:MIX_WIDTH, :]) + _dot(om, wo_ref[MIX_WIDTH:, :])
    y_ref[...] = y
    xn = _rmsnorm_rows(y, g_ref[...])
    xn_ref[...] = xn
    xh = xn.astype(BF16)
    xl = (xn - xh.astype(F32)).astype(BF16)
    wh, wl = wrh_ref[...], wrl_ref[...]
    logit = _dot(xh, wh) + _dot(xl, wh) + _dot(xh, wl) + br_ref[...]

    tm = logit.shape[0]
    lane = _iota((tm, LANES), 1)
    lg = jnp.where(lane < N_GROUPS, logit, -jnp.inf)
    mg = jnp.max(lg, axis=-1, keepdims=True)
    g_idx = jnp.min(jnp.where(lg == mg, lane, LANES), axis=-1, keepdims=True)
    g_w = 1.0 / jnp.sum(jnp.exp(lg - mg), axis=-1, keepdims=True)
    in_group = (lane >= ROUTE_LANE0) & (((lane - ROUTE_LANE0) >> 3) == g_idx) & (lane < ROUTE_LANE0 + N_EXPERTS)
    le = jnp.where(in_group, logit, -jnp.inf)
    v1 = jnp.max(le, axis=-1, keepdims=True)
    i1 = jnp.min(jnp.where(le == v1, lane, LANES), axis=-1, keepdims=True)
    le2 = jnp.where(lane == i1, -jnp.inf, le)
    v2 = jnp.max(le2, axis=-1, keepdims=True)
    i2 = jnp.min(jnp.where(le2 == v2, lane, LANES), axis=-1, keepdims=True)
    e21 = jnp.exp(v2 - v1)
    w1 = 1.0 / (1.0 + e21)
    w2 = e21 * w1

    hit1, hit2 = lane == i1, lane == i2
    onehot = jnp.where(hit1 | hit2, 1.0, 0.0)
    lower = jnp.where(_iota((tm, tm), 1) < _iota((tm, tm), 0), 1.0, 0.0).astype(BF16)
    before = _dot(lower, onehot.astype(BF16)) + run_ref[...]
    r1 = jnp.sum(jnp.where(hit1, before, 0.0), axis=-1, keepdims=True)
    r2 = jnp.sum(jnp.where(hit2, before, 0.0), axis=-1, keepdims=True)
    run_ref[...] = run_ref[...] + jnp.sum(onehot, axis=0, keepdims=True)
    cnt_ref[...] = run_ref[...]

    info = jnp.zeros((tm, LANES), F32)
    fields = ((i1 - ROUTE_LANE0).astype(F32), (i2 - ROUTE_LANE0).astype(F32), g_w * w1, g_w * w2, r1, r2)
    for j, val in enumerate(fields):
        info = jnp.where(lane == j, val, info)
    info_ref[...] = info


def _out_router(x, o_p, om_p, o_s, om_s, wo_bf16, g_ffn, wr_hi, wr_lo, b_route):
    n, d = x.shape
    tm = ROW_TILE
    pt = o_p.shape[0] // tm
    row = lambda w: pl.BlockSpec((tm, w), lambda i: (i, 0))
    prow = lambda w: pl.BlockSpec((tm, w), lambda i: (jnp.minimum(i, pt - 1), 0))
    srow = lambda w: pl.BlockSpec((tm, w), lambda i: (jnp.maximum(i - pt, 0), 0))
    full = lambda a: pl.BlockSpec(a.shape, lambda i: (0, 0))
    g2 = g_ffn.reshape(1, d)
    return pl.pallas_call(
        functools.partial(_out_router_body, prompt_tiles=pt), grid=(n // tm,),
        in_specs=[row(d), prow(MIX_WIDTH), prow(MEM_WIDTH), srow(MIX_WIDTH), srow(MEM_WIDTH),
                  full(wo_bf16), full(g2), full(wr_hi), full(wr_lo), full(b_route)],
        out_specs=[row(d), row(d), row(LANES), pl.BlockSpec((1, LANES), lambda i: (0, 0))],
        out_shape=[jax.ShapeDtypeStruct((n, d), F32), jax.ShapeDtypeStruct((n, d), F32),
                   jax.ShapeDtypeStruct((n, LANES), F32), jax.ShapeDtypeStruct((1, LANES), F32)],
        scratch_shapes=[pltpu.VMEM((1, LANES), F32)],
        compiler_params=_cparams("arbitrary"), name="out_router",
    )(x, o_p, om_p, o_s, om_s, wo_bf16, g2, wr_hi, wr_lo, b_route)


def _expert_body(be_ref, nused_ref, x_ref, wgu_ref, wdn_ref, y_ref, wgu_s, wdn_s):
    i = pl.program_id(0)
    prev = be_ref[jnp.maximum(i - 1, 0)]

    @pl.when((i == 0) | (be_ref[i] != prev))
    def _():
        wgu_s[...] = wgu_ref[...].astype(BF16)
        wdn_s[...] = wdn_ref[...].astype(BF16)

    @pl.when(i < nused_ref[0])
    def _():
        h = _dot(x_ref[...].astype(BF16), wgu_s[...])
        a, u = h[:, :EXPERT_FF], h[:, EXPERT_FF:]
        act = a * (1.0 / (1.0 + jnp.exp(-a))) * u
        y_ref[...] = _dot(act.astype(BF16), wdn_s[...]).astype(BF16)

    @pl.when(i >= nused_ref[0])
    def _():
        y_ref[...] = jnp.zeros_like(y_ref)


def _experts(layer, xs, block_expert, n_used, w_gu, w_dn):
    rows, d = xs.shape
    nb = rows // MOE_ROWS
    return pl.pallas_call(
        _expert_body,
        grid_spec=pltpu.PrefetchScalarGridSpec(
            num_scalar_prefetch=2, grid=(nb,),
            in_specs=[pl.BlockSpec((MOE_ROWS, d), lambda i, be, nu: (i, 0)),
                      pl.BlockSpec((None, None, d, 2 * EXPERT_FF), lambda i, be, nu: (layer, be[i], 0, 0)),
                      pl.BlockSpec((None, None, EXPERT_FF, d), lambda i, be, nu: (layer, be[i], 0, 0))],
            out_specs=pl.BlockSpec((MOE_ROWS, d), lambda i, be, nu: (i, 0)),
            scratch_shapes=[pltpu.VMEM((d, 2 * EXPERT_FF), BF16), pltpu.VMEM((EXPERT_FF, d), BF16)]),
        out_shape=jax.ShapeDtypeStruct((rows, d), BF16),
        compiler_params=_cparams("arbitrary"), name="experts",
    )(block_expert, n_used, xs, w_gu, w_dn)


def _moe(layer, y, xn, info, counts, w_gu, w_dn):
    n = y.shape[0]
    gate = info[:, 2:4]
    cnt = counts[0, ROUTE_LANE0:ROUTE_LANE0 + N_EXPERTS].astype(jnp.int32)
    padded = (cnt + MOE_ROWS - 1) // MOE_ROWS * MOE_ROWS
    pad_end = jnp.cumsum(padded)
    pad_start = pad_end - padded
    dest = [pad_start[info[:, k].astype(jnp.int32)] + info[:, 4 + k].astype(jnp.int32) for k in range(2)]
    nb = -(-(2 * n + N_EXPERTS * (MOE_ROWS - 1)) // MOE_ROWS)
    block_row0 = jnp.arange(nb, dtype=jnp.int32) * MOE_ROWS
    block_expert = jnp.minimum(jnp.sum((pad_end[None, :] <= block_row0[:, None]).astype(jnp.int32), axis=1),
                               N_EXPERTS - 1)
    n_used = (pad_end[-1:] // MOE_ROWS).astype(jnp.int32)
    tok = jnp.arange(n, dtype=jnp.int32)
    row_tok = jnp.zeros((nb * MOE_ROWS,), jnp.int32).at[jnp.concatenate(dest)].set(jnp.concatenate([tok, tok]))
    cuts = [j * nb // GATHER_SPLIT * MOE_ROWS for j in range(GATHER_SPLIT + 1)]
    xs = jnp.concatenate([xn[row_tok[a:b]] for a, b in zip(cuts[:-1], cuts[1:])], axis=0)
    ys = _experts(layer, xs, block_expert, n_used, w_gu, w_dn)
    return y + gate[:, 0:1] * ys[dest[0]].astype(F32) + gate[:, 1:2] * ys[dest[1]].astype(F32)


def _final_norm_body(x_ref, g_ref, o_ref):
    o_ref[...] = _rmsnorm_rows(x_ref[...], g_ref[...])


def _final_norm(x, row0, n_rows, g):
    d = x.shape[1]
    tm = ROW_TILE
    t0 = row0 // tm
    return pl.pallas_call(
        _final_norm_body, grid=(n_rows // tm,),
        in_specs=[pl.BlockSpec((tm, d), lambda i: (t0 + i, 0)), pl.BlockSpec((1, d), lambda i: (0, 0))],
        out_specs=pl.BlockSpec((tm, d), lambda i: (i, 0)), out_shape=jax.ShapeDtypeStruct((n_rows, d), F32),
        compiler_params=_cparams("parallel"), name="final_norm",
    )(x, g.reshape(1, d))


def kernel(x_prompt, x_sample, mem_prompt, cache_k, cache_v, cache_mem_k, cache_mem_v, page_table, norm_attn, norm_mem, norm_ffn, norm_final, w_in, w_mem_kv, w_out, diff_lambda, diff_subln, w_router_group, b_router_group, w_router_expert, b_router_expert, w_expert_gu, w_expert_down):
    batch, seq, d = x_prompt.shape
    db, dt, _ = x_sample.shape
    depth = w_in.shape[0]
    n_mem = mem_prompt.shape[1]
    past_len = page_table.shape[1] * cache_k.shape[2]
    n_p, n_s = batch * seq, db * dt
    n = n_p + n_s
    assert seq % ROW_TILE == 0 and n_s % ROW_TILE == 0 and ROW_TILE % dt == 0 and seq % DIFF_TILE == 0

    x = jnp.concatenate([x_prompt.reshape(n_p, d), x_sample.reshape(n_s, d)], axis=0)
    seq_tiles = seq // ROW_TILE
    pos = jnp.concatenate([jnp.arange(seq, dtype=jnp.int32),
                           past_len + jnp.arange(ROW_TILE, dtype=jnp.int32) % dt])
    rot_tables = _rotary_tables(pos)

    mem_k, mem_v = _memkv(mem_prompt.reshape(batch * n_mem, d), norm_mem, w_mem_kv)
    mem_k = mem_k.reshape(depth, batch, n_mem, MEM_WIDTH)
    mem_v = mem_v.reshape(depth, batch, n_mem, MEM_WIDTH)

    kv_p = [jnp.zeros((depth, n_p, MIX_WIDTH), F32)] * 2
    kv_s = [jnp.zeros((depth, n_s, MIX_WIDTH), F32)] * 2
    for i in range(depth):
        is_diff = i % 2 == 1
        diff_params = None
        q_scale = QK_SCALE
        if is_diff:
            lam_init = 0.8 - 0.6 * math.exp(-0.3 * i)
            diff_params = (diff_lambda[i // 2], diff_subln[i // 2], lam_init)
            q_scale = QK_SCALE * LOG2E
        w_in_i = w_in[i].astype(BF16)
        rot_p = (rot_tables, 0, seq_tiles) if is_diff else None
        rot_s = (rot_tables, seq_tiles, 1) if is_diff else None
        *kv_p, qb, kb, vb, qm = _proj(x, 0, n_p, norm_attn[i], w_in_i, q_scale, rot_p, i, kv_p)
        *kv_s, qb_s, kb_s, vb_s, qm_s = _proj(x, n_p, n_s, norm_attn[i], w_in_i, q_scale, rot_s, i, kv_s)

        o = _prompt_attention(qb, kb, vb, batch, seq, diff_params)
        om = _mem_prompt(qm, mem_k[i], mem_v[i], batch, seq)
        s3 = lambda a: a.reshape(db, dt, a.shape[-1])
        o_s, om_s = _sample_attention(i, s3(qb_s), s3(kb_s), s3(vb_s), s3(qm_s), cache_k, cache_v,
                                      cache_mem_k, cache_mem_v, page_table, diff_params)
        o_s = o_s.reshape(n_s, MIX_WIDTH)
        om_s = om_s.reshape(n_s, MEM_WIDTH)

        w_route = jnp.zeros((d, LANES), F32)
        w_route = w_route.at[:, :N_GROUPS].set(w_router_group[i])
        w_route = w_route.at[:, ROUTE_LANE0:ROUTE_LANE0 + N_EXPERTS].set(w_router_expert[i])
        wr_hi = w_route.astype(BF16)
        wr_lo = (w_route - wr_hi.astype(F32)).astype(BF16)
        b_route = jnp.zeros((1, LANES), F32)
        b_route = b_route.at[0, :N_GROUPS].set(b_router_group[i])
        b_route = b_route.at[0, ROUTE_LANE0:ROUTE_LANE0 + N_EXPERTS].set(b_router_expert[i].reshape(-1))
        y, xn, info, counts = _out_router(x, o, om, o_s, om_s, w_out[i].astype(BF16), norm_ffn[i],
                                          wr_hi, wr_lo, b_route)
        x = _moe(i, y, xn, info, counts, w_expert_gu, w_expert_down)

    y_p = _final_norm(x, 0, n_p, norm_final).reshape(batch, seq, d)
    y_s = _final_norm(x, n_p, n_s, norm_final).reshape(db, dt, d)
    prompt = lambda a: a.reshape(depth, batch, seq, MIX_WIDTH)
    sample = lambda a: a.reshape(depth, db, dt, MIX_WIDTH)
    return (y_p, y_s, prompt(kv_p[0]), prompt(kv_p[1]), mem_k, mem_v, sample(kv_s[0]), sample(kv_s[1]))
```

```python
import functools
import math

import jax
import jax.numpy as jnp
from jax import lax
from jax.experimental import pallas as pl
from jax.experimental.pallas import tpu as pltpu

F32 = jnp.float32
BF16 = jnp.bfloat16

HEAD_DIM = 64
SB_HEADS = 12
DIFF_HEADS = 6
MEM_HEADS = 4
MIX_WIDTH = SB_HEADS * HEAD_DIM
MEM_WIDTH = MEM_HEADS * HEAD_DIM
ROT_DIM = HEAD_DIM // 4
ROPE_THETA = 500000.0
N_GROUPS = 8
EXPERTS_PER_GROUP = 8
N_EXPERTS = N_GROUPS * EXPERTS_PER_GROUP
EXPERT_FF = 256
RMS_EPS = 1e-6
QK_SCALE = HEAD_DIM ** -0.5

LANES = 128
ROW_TILE = 512
ATTN_TILE = 256
DIFF_TILE = 512
LOG2E = 1.4426950408889634
MOE_ROWS = 128
GATHER_SPLIT = 3
NEG_BIG = -1e30
SB_DEAD = -104.0
VMEM_LIMIT = 48 * 1024 * 1024
ROUTE_LANE0 = N_GROUPS


def _cparams(*sem):
    return pltpu.CompilerParams(dimension_semantics=sem, vmem_limit_bytes=VMEM_LIMIT)


def _rmsnorm_rows(x, g):
    ms = jnp.mean(x * x, axis=-1, keepdims=True)
    return x * lax.rsqrt(ms + RMS_EPS) * g


def _dot(a, b):
    return jnp.dot(a, b, preferred_element_type=F32)


def _dot_nt(a, b):
    return lax.dot_general(a, b, (((1,), (1,)), ((), ())), preferred_element_type=F32)


def _iota(shape, axis):
    return lax.broadcasted_iota(jnp.int32, shape, axis)


def _proj_body(rotary, q_scale, x_ref, g_ref, w_ref, *rest):
    if rotary:
        cos_ref, sa_ref, sb_ref = rest[:3]
        rest = rest[3:]
    kf_ref, vf_ref, qb_ref, kb_ref, vb_ref, qm_ref = rest[-6:]
    xn = _rmsnorm_rows(x_ref[...], g_ref[...]).astype(BF16)
    q = _dot(xn, w_ref[:, 0:MIX_WIDTH])
    k = _dot(xn, w_ref[:, MIX_WIDTH:2 * MIX_WIDTH])
    v = _dot(xn, w_ref[:, 2 * MIX_WIDTH:3 * MIX_WIDTH])
    qm = _dot(xn, w_ref[:, 3 * MIX_WIDTH:])
    if rotary:
        c, sa, sb = cos_ref[...], sa_ref[...], sb_ref[...]

        def rot(t):
            outs = []
            for j in range(MIX_WIDTH // LANES):
                s = t[:, j * LANES:(j + 1) * LANES]
                outs.append(s * c + pltpu.roll(s, LANES - ROT_DIM // 2, 1) * sa
                            + pltpu.roll(s, ROT_DIM // 2, 1) * sb)
            return jnp.concatenate(outs, axis=1)

        q = rot(q)
        k = rot(k)
    kf_ref[...] = k
    vf_ref[...] = v
    qb_ref[...] = (q * q_scale).astype(BF16)
    kb_ref[...] = k.astype(BF16)
    vb_ref[...] = v.astype(BF16)
    qm_ref[...] = (qm * QK_SCALE).astype(BF16)


def _proj(x, row0, n_rows, g, w_bf16, q_scale, rot, layer, kv_prev):
    d = x.shape[1]
    tm = ROW_TILE
    t0 = row0 // tm
    in_specs = [pl.BlockSpec((tm, d), lambda i: (t0 + i, 0)), pl.BlockSpec((1, d), lambda i: (0, 0)),
                pl.BlockSpec(w_bf16.shape, lambda i: (0, 0))]
    args = [x, g.reshape(1, d), w_bf16]
    if rot is not None:
        tables, first, nblk = rot
        in_specs += [pl.BlockSpec((tm, LANES), lambda i: (first + i % nblk, 0))] * 3
        args += list(tables)
    aliases = {len(args): 0, len(args) + 1: 1}
    in_specs += [pl.BlockSpec(memory_space=pl.ANY)] * 2
    args += list(kv_prev)
    row = lambda w: pl.BlockSpec((tm, w), lambda i: (i, 0))
    kv_spec = pl.BlockSpec((None, tm, MIX_WIDTH), lambda i: (layer, i, 0))
    out_shape = [jax.ShapeDtypeStruct(kv_prev[0].shape, F32)] * 2 + \
                [jax.ShapeDtypeStruct((n_rows, MIX_WIDTH), BF16)] * 3 + \
                [jax.ShapeDtypeStruct((n_rows, MEM_WIDTH), BF16)]
    out_specs = [kv_spec] * 2 + [row(MIX_WIDTH)] * 3 + [row(MEM_WIDTH)]
    return pl.pallas_call(
        functools.partial(_proj_body, rot is not None, q_scale),
        grid=(n_rows // tm,), in_specs=in_specs, out_specs=out_specs, out_shape=out_shape,
        input_output_aliases=aliases,
        compiler_params=_cparams("parallel"), name="proj_rot" if rot is not None else "proj",
    )(*args)


def _rotary_tables(pos):
    half = ROT_DIM // 2
    lane = jnp.arange(LANES) % HEAD_DIM
    inv_freq = 1.0 / (ROPE_THETA ** ((lane % half).astype(F32) * 2.0 / ROT_DIM))
    ang = pos.astype(F32)[:, None] * inv_freq[None, :]
    cos, sin = jnp.cos(ang), jnp.sin(ang)
    c = jnp.where(lane[None, :] < ROT_DIM, cos, 1.0)
    sa = jnp.where(lane[None, :] < half, -sin, 0.0)
    sb = jnp.where((lane[None, :] >= half) & (lane[None, :] < ROT_DIM), sin, 0.0)
    return c, sa, sb


def _memkv_body(mem_ref, g_ref, w_ref, k_ref, v_ref):
    xn = _rmsnorm_rows(mem_ref[...], g_ref[...]).astype(BF16)
    kv = _dot(xn, w_ref[...].astype(BF16))
    k_ref[...] = kv[:, :MEM_WIDTH]
    v_ref[...] = kv[:, MEM_WIDTH:]


def _memkv(mem2d, norm_mem, w_mem_kv):
    depth, d = norm_mem.shape
    m = mem2d.shape[0]
    out = jax.ShapeDtypeStruct((depth, m, MEM_WIDTH), F32)
    ospec = pl.BlockSpec((None, m, MEM_WIDTH), lambda l: (l, 0, 0))
    return pl.pallas_call(
        _memkv_body, grid=(depth,),
        in_specs=[pl.BlockSpec((m, d), lambda l: (0, 0)),
                  pl.BlockSpec((None, 1, d), lambda l: (l, 0, 0)),
                  pl.BlockSpec((None, d, 2 * MEM_WIDTH), lambda l: (l, 0, 0))],
        out_specs=[ospec, ospec], out_shape=[out, out],
        compiler_params=_cparams("parallel"), name="mem_kv",
    )(mem2d, norm_mem.reshape(depth, 1, d), w_mem_kv)


def _neg_upper(tk):
    return jnp.where(_iota((tk, tk), 0) > _iota((tk, tk), 1), -1.0, 0.0).astype(BF16)


def _sb_chunk(qh, ks, vs, uneg, carry, acc, mask):
    z = _dot_nt(qh, ks)
    sp = jnp.maximum(z, 0.0) + jnp.log(1.0 + jnp.exp(-jnp.abs(z)))
    spm = sp if mask is None else jnp.where(mask, sp, 0.0)
    hi = spm.astype(BF16)
    lo = (spm - hi.astype(F32)).astype(BF16)
    between = _dot(hi, uneg) + _dot(lo, uneg) + carry
    a = jnp.exp(z - sp + between)
    if mask is not None:
        a = jnp.where(mask, a, 0.0)
    acc = acc + _dot(a.astype(BF16), vs)
    carry = carry - jnp.sum(spm, axis=-1, keepdims=True)
    return carry, acc


def _sm_chunk(qh, ks, vs, m, l, acc, mask):
    s = _dot_nt(qh, ks)
    if mask is not None:
        s = jnp.where(mask, s, NEG_BIG)
    m_new = jnp.maximum(m, jnp.max(s, axis=-1, keepdims=True))
    alpha = jnp.exp2(m - m_new)
    p = jnp.exp2(s - jnp.concatenate([m_new] * (s.shape[1] // LANES), axis=1))
    l = alpha * l + jnp.sum(p, axis=-1, keepdims=True)
    acc = alpha * acc + _dot(p.astype(BF16), vs)
    return m_new, l, acc


def _diff_lambda(lam_ref, lam_init):
    lv = lam_ref[...]
    s1 = jnp.sum(lv[0:1] * lv[1:2], axis=-1, keepdims=True)
    s2 = jnp.sum(lv[2:3] * lv[3:4], axis=-1, keepdims=True)
    return jnp.exp(s1) - jnp.exp(s2) + lam_init


def _head_rmsnorm(o, gain, lam_init):
    return _rmsnorm_rows(o, gain) * (1.0 - lam_init)


def _mem_attention(qbd, mk, mv):
    s = _dot_nt(qbd, mk)
    p = jnp.exp(s - jnp.max(s, axis=-1, keepdims=True))
    o = _dot(p.astype(BF16), mv)
    return o / jnp.sum(p, axis=-1, keepdims=True)


def _stack_lane_halves(q, lane):
    return jnp.concatenate([jnp.where(lane < HEAD_DIM, q, 0.0), jnp.where(lane >= HEAD_DIM, q, 0.0)],
                           axis=0).astype(BF16)


def _sb_prompt_body(q_ref, k_ref, v_ref, o_ref, acc_ref, carry_ref, *, t):
    i = pl.program_id(2)
    lane = _iota((t, LANES), 1)
    qq = _stack_lane_halves(q_ref[...].astype(F32), lane)
    uneg = _neg_upper(t)
    causal = _iota((2 * t, t), 1) < (_iota((2 * t, t), 0) & (t - 1))

    def process(c, mask):
        start = pl.multiple_of(c * t, t)
        carry, acc = _sb_chunk(qq, k_ref[pl.ds(start, t), :], v_ref[pl.ds(start, t), :], uneg,
                               carry_ref[...], acc_ref[...], mask)
        carry_ref[...] = carry
        acc_ref[...] = acc

    def alive():
        return (jnp.max(carry_ref[...]) > SB_DEAD).astype(jnp.int32)

    carry_ref[...] = jnp.zeros_like(carry_ref)
    acc_ref[...] = jnp.zeros_like(acc_ref)

    @pl.when(i == 0)
    def _():
        process(i, causal)

    @pl.when(i > 0)
    def _():
        process(i, causal)
        process(i - 1, None)

    def body(st):
        process(st[0], None)
        return st[0] - 1, alive()

    lax.while_loop(lambda st: (st[0] >= 0) & (st[1] > 0), body, (i - 2, alive()))
    o_ref[...] = jnp.where(lane < HEAD_DIM, acc_ref[0:t], acc_ref[t:2 * t]).astype(BF16)


def _diff_prompt_body(lam_ref, gain_ref, q_ref, k_ref, v_ref, o_ref, m_ref, l_ref, acc_ref, *, t, lam_init):
    i = pl.program_id(2)
    qq = _stack_lane_halves(q_ref[...].astype(F32), _iota((t, LANES), 1))

    def process(c, mask):
        start = pl.multiple_of(c * t, t)
        m, l, acc = _sm_chunk(qq, k_ref[pl.ds(start, t), :], v_ref[pl.ds(start, t), :],
                              m_ref[...], l_ref[...], acc_ref[...], mask)
        m_ref[...] = m
        l_ref[...] = l
        acc_ref[...] = acc

    m_ref[...] = jnp.full_like(m_ref, NEG_BIG)
    l_ref[...] = jnp.zeros_like(l_ref)
    acc_ref[...] = jnp.zeros_like(acc_ref)

    def body(c, carry):
        process(c, None)
        return carry

    lax.fori_loop(0, i, body, 0)
    process(i, _iota((2 * t, t), 1) <= (_iota((2 * t, t), 0) & (t - 1)))
    lam = _diff_lambda(lam_ref, lam_init)
    o = acc_ref[0:t] / l_ref[0:t] - lam * (acc_ref[t:2 * t] / l_ref[t:2 * t])
    o_ref[...] = _head_rmsnorm(o, gain_ref[...], lam_init).astype(BF16)


def _prompt_attention(qb, kb, vb, batch, seq, diff_params):
    n_rows = batch * seq
    t = min(ATTN_TILE if diff_params is None else DIFF_TILE, seq)
    nq = seq // t
    nh = MIX_WIDTH // LANES
    qspec = pl.BlockSpec((t, LANES), lambda b, h, i: (b * nq + i, h))
    kvspec = pl.BlockSpec((seq, LANES), lambda b, h, i: (b, h))
    out_shape = jax.ShapeDtypeStruct((n_rows, MIX_WIDTH), BF16)
    if diff_params is None:
        body = functools.partial(_sb_prompt_body, t=t)
        in_specs, args = [qspec, kvspec, kvspec], [qb, kb, vb]
        scratch = [pltpu.VMEM((2 * t, LANES), F32), pltpu.VMEM((2 * t, 1), F32)]
        name = "sb_prompt"
    else:
        lam_vecs, gain, lam_init = diff_params
        body = functools.partial(_diff_prompt_body, t=t, lam_init=lam_init)
        in_specs = [pl.BlockSpec((4, HEAD_DIM), lambda b, h, i: (0, 0)),
                    pl.BlockSpec((1, LANES), lambda b, h, i: (0, 0)), qspec, kvspec, kvspec]
        args = [lam_vecs, gain.reshape(1, LANES), qb, kb, vb]
        scratch = [pltpu.VMEM((2 * t, LANES), F32)] * 3
        name = "diff_prompt"
    return pl.pallas_call(
        body, grid=(batch, nh, nq), in_specs=in_specs, out_specs=qspec, out_shape=out_shape,
        scratch_shapes=scratch, compiler_params=_cparams("parallel", "parallel", "arbitrary"), name=name,
    )(*args)


def _mem_prompt_body(q_ref, mk_ref, mv_ref, o_ref):
    q = q_ref[...].astype(F32)
    mk = mk_ref[...].astype(BF16)
    mv = mv_ref[...].astype(BF16)
    lane_head = _iota(q.shape, 1) // HEAD_DIM
    out = jnp.zeros(q.shape, F32)
    for h in range(MEM_HEADS):
        o = _mem_attention(jnp.where(lane_head == h, q, 0.0).astype(BF16), mk, mv)
        out = jnp.where(lane_head == h, o, out)
    o_ref[...] = out.astype(BF16)


def _mem_prompt(qm, mk, mv, batch, seq):
    n_rows = batch * seq
    t = min(ROW_TILE, seq)
    nq = seq // t
    n_mem = mk.shape[1]
    qspec = pl.BlockSpec((t, MEM_WIDTH), lambda b, i: (b * nq + i, 0))
    mspec = pl.BlockSpec((None, n_mem, MEM_WIDTH), lambda b, i: (b, 0, 0))
    return pl.pallas_call(
        _mem_prompt_body, grid=(batch, nq), in_specs=[qspec, mspec, mspec], out_specs=qspec,
        out_shape=jax.ShapeDtypeStruct((n_rows, MEM_WIDTH), BF16),
        compiler_params=_cparams("parallel", "parallel"), name="mem_prompt",
    )(qm, mk, mv)


def _block_diag_queries(q, n_blocks, block_lanes):
    dt = q.shape[0]
    qt = jnp.concatenate([q] * n_blocks, axis=0)
    keep = (_iota(qt.shape, 0) // dt) == (_iota(qt.shape, 1) // block_lanes)
    return jnp.where(keep, qt, 0.0).astype(BF16)


def _fold_blocks(full, dt, block_lanes):
    keep = (_iota(full.shape, 0) // dt) == (_iota(full.shape, 1) // block_lanes)
    sel = jnp.where(keep, full, 0.0)
    out = sel[0:dt]
    for j in range(1, full.shape[0] // dt):
        out = out + sel[j * dt:(j + 1) * dt]
    return out


def _sample_body(pt_ref, *refs, n_pages, page, dt, diff, lam_init):
    del pt_ref
    if diff:
        lam_ref, gain_ref = refs[:2]
        refs = refs[2:]
    q_ref, kn_ref, vn_ref, qm_ref, mk_ref, mv_ref = refs[:6]
    k_pages = refs[6:6 + n_pages]
    v_pages = refs[6 + n_pages:6 + 2 * n_pages]
    o_ref, om_ref, kall, vall = refs[6 + 2 * n_pages:]

    n_maps = MIX_WIDTH // HEAD_DIM
    rows = n_maps * dt
    past = n_pages * page
    qbd = _block_diag_queries(q_ref[...].astype(F32), n_maps, HEAD_DIM)
    for p in range(n_pages):
        kall[p * page:(p + 1) * page, :] = k_pages[p][...].astype(BF16)
        vall[p * page:(p + 1) * page, :] = v_pages[p][...].astype(BF16)
    pad = jnp.zeros((page - dt, MIX_WIDTH), F32)
    kall[past:, :] = jnp.concatenate([kn_ref[...].astype(F32), pad], axis=0).astype(BF16)
    vall[past:, :] = jnp.concatenate([vn_ref[...].astype(F32), pad], axis=0).astype(BF16)
    tpos = _iota((rows, page), 0) % dt
    col = _iota((rows, page), 1)
    z = _dot_nt(qbd, kall[...])
    chunks = [z[:, c * page:(c + 1) * page] for c in range(n_pages + 1)]

    if not diff:
        new_ok = col < tpos
        sp = [jnp.maximum(zc, 0.0) + jnp.log(1.0 + jnp.exp(-jnp.abs(zc))) for zc in chunks]
        spm = sp[:-1] + [jnp.where(new_ok, sp[-1], 0.0)]
        stacked = jnp.concatenate(spm, axis=0)
        hi = stacked.astype(BF16)
        lo = (stacked - hi.astype(F32)).astype(BF16)
        uneg = _neg_upper(page)
        later = _dot(hi, uneg) + _dot(lo, uneg)
        carry = jnp.zeros((rows, 1), F32)
        a = [None] * (n_pages + 1)
        for c in reversed(range(n_pages + 1)):
            a[c] = jnp.exp(chunks[c] - sp[c] + later[c * rows:(c + 1) * rows] + carry)
            carry = carry - jnp.sum(spm[c], axis=-1, keepdims=True)
        a[-1] = jnp.where(new_ok, a[-1], 0.0)
        acc = _dot(jnp.concatenate(a, axis=1).astype(BF16), vall[...])
        o = _fold_blocks(acc, dt, HEAD_DIM)
    else:
        s = jnp.concatenate(chunks[:-1] + [jnp.where(col <= tpos, chunks[-1], NEG_BIG)], axis=1)
        p = jnp.exp2(s - jnp.max(s, axis=-1, keepdims=True))
        l = jnp.sum(p, axis=-1, keepdims=True)
        acc = _dot(p.astype(BF16), vall[...])
        lam = _diff_lambda(lam_ref, lam_init)
        second_map = ((_iota((rows, 1), 0) // dt) % 2) == 1
        weight = jnp.where(second_map, -lam, 1.0) / l
        o = _fold_blocks(acc * weight, 2 * dt, 2 * HEAD_DIM)
        o = o[0:dt] + o[dt:2 * dt]
        gain = gain_ref[...]
        o = jnp.concatenate(
            [_head_rmsnorm(o[:, j * LANES:(j + 1) * LANES], gain, lam_init) for j in range(DIFF_HEADS)], axis=1)
    o_ref[...] = o.astype(BF16)

    qmbd = _block_diag_queries(qm_ref[...].astype(F32), MEM_HEADS, HEAD_DIM)
    om = _mem_attention(qmbd, mk_ref[...].astype(BF16), mv_ref[...].astype(BF16))
    om_ref[...] = _fold_blocks(om, dt, HEAD_DIM).astype(BF16)


def _sample_attention(layer, q3, kn3, vn3, qm3, cache_k, cache_v, cache_mem_k, cache_mem_v, page_table,
                      diff_params):
    db, dt, _ = q3.shape
    n_pages = page_table.shape[1]
    page = cache_k.shape[2]
    n_mem = cache_mem_k.shape[2]
    diff = diff_params is not None

    def req(w):
        return pl.BlockSpec((None, dt, w), lambda b, pt: (b, 0, 0))

    def page_spec(p):
        return pl.BlockSpec((None, None, page, MIX_WIDTH), lambda b, pt: (layer, pt[b * n_pages + p], 0, 0))

    mem_spec = pl.BlockSpec((None, None, n_mem, MEM_WIDTH), lambda b, pt: (layer, b, 0, 0))
    in_specs = [req(MIX_WIDTH)] * 3 + [req(MEM_WIDTH), mem_spec, mem_spec]
    in_specs += [page_spec(p) for p in range(n_pages)] * 2
    args = [q3, kn3, vn3, qm3, cache_mem_k, cache_mem_v] + [cache_k] * n_pages + [cache_v] * n_pages
    lam_init = 0.0
    if diff:
        lam_vecs, gain, lam_init = diff_params
        in_specs = [pl.BlockSpec((4, HEAD_DIM), lambda b, pt: (0, 0)),
                    pl.BlockSpec((1, LANES), lambda b, pt: (0, 0))] + in_specs
        args = [lam_vecs, gain.reshape(1, LANES)] + args
    body = functools.partial(_sample_body, n_pages=n_pages, page=page, dt=dt, diff=diff, lam_init=lam_init)
    return pl.pallas_call(
        body,
        grid_spec=pltpu.PrefetchScalarGridSpec(
            num_scalar_prefetch=1, grid=(db,), in_specs=in_specs,
            out_specs=[req(MIX_WIDTH), req(MEM_WIDTH)],
            scratch_shapes=[pltpu.VMEM(((n_pages + 1) * page, MIX_WIDTH), BF16)] * 2),
        out_shape=[jax.ShapeDtypeStruct((db, dt, MIX_WIDTH), BF16),
                   jax.ShapeDtypeStruct((db, dt, MEM_WIDTH), BF16)],
        compiler_params=_cparams("arbitrary"), name="diff_sample" if diff else "sb_sample",
    )(page_table.reshape(-1), *args)


def _out_router_body(x_ref, op_ref, omp_ref, os_ref, oms_ref, wo_ref, g_ref, wrh_ref, wrl_ref, br_ref,
                     y_ref, xn_ref, info_ref, cnt_ref, run_ref, *, prompt_tiles):
    @pl.when(pl.program_id(0) == 0)
    def _():
        run_ref[...] = jnp.zeros_like(run_ref)

    is_prompt = pl.program_id(0) < prompt_tiles
    o = jnp.where(is_prompt, op_ref[...], os_ref[...])
    om = jnp.where(is_prompt, omp_ref[...], oms_ref[...])
    y = x_ref[...] + _dot(o, wo_ref[0:MIX_WIDTH, :]) + _dot(om, wo_ref[MIX_WIDTH:, :])
    y_ref[...] = y
    xn = _rmsnorm_rows(y, g_ref[...])
    xn_ref[...] = xn
    xh = xn.astype(BF16)
    xl = (xn - xh.astype(F32)).astype(BF16)
    wh, wl = wrh_ref[...], wrl_ref[...]
    logit = _dot(xh, wh) + _dot(xl, wh) + _dot(xh, wl) + br_ref[...]

    tm = logit.shape[0]
    lane = _iota((tm, LANES), 1)
    lg = jnp.where(lane < N_GROUPS, logit, -jnp.inf)
    mg = jnp.max(lg, axis=-1, keepdims=True)
    g_idx = jnp.min(jnp.where(lg == mg, lane, LANES), axis=-1, keepdims=True)
    g_w = 1.0 / jnp.sum(jnp.exp(lg - mg), axis=-1, keepdims=True)
    in_group = (lane >= ROUTE_LANE0) & (((lane - ROUTE_LANE0) >> 3) == g_idx) & (lane < ROUTE_LANE0 + N_EXPERTS)
    le = jnp.where(in_group, logit, -jnp.inf)
    v1 = jnp.max(le, axis=-1, keepdims=True)
    i1 = jnp.min(jnp.where(le == v1, lane, LANES), axis=-1, keepdims=True)
    le2 = jnp.where(lane == i1, -jnp.inf, le)
    v2 = jnp.max(le2, axis=-1, keepdims=True)
    i2 = jnp.min(jnp.where(le2 == v2, lane, LANES), axis=-1, keepdims=True)
    e21 = jnp.exp(v2 - v1)
    w1 = 1.0 / (1.0 + e21)
    w2 = e21 * w1

    hit1, hit2 = lane == i1, lane == i2
    onehot = jnp.where(hit1 | hit2, 1.0, 0.0)
    lower = jnp.where(_iota((tm, tm), 1) < _iota((tm, tm), 0), 1.0, 0.0).astype(BF16)
    before = _dot(lower, onehot.astype(BF16)) + run_ref[...]
    r1 = jnp.sum(jnp.where(hit1, before, 0.0), axis=-1, keepdims=True)
    r2 = jnp.sum(jnp.where(hit2, before, 0.0), axis=-1, keepdims=True)
    run_ref[...] = run_ref[...] + jnp.sum(onehot, axis=0, keepdims=True)
    cnt_ref[...] = run_ref[...]

    info = jnp.zeros((tm, LANES), F32)
    fields = ((i1 - ROUTE_LANE0).astype(F32), (i2 - ROUTE_LANE0).astype(F32), g_w * w1, g_w * w2, r1, r2)
    for j, val in enumerate(fields):
        info = jnp.where(lane == j, val, info)
    info_ref[...] = info


def _out_router(x, o_p, om_p, o_s, om_s, wo_bf16, g_ffn, wr_hi, wr_lo, b_route):
    n, d = x.shape
    tm = ROW_TILE
    pt = o_p.shape[0] // tm
    row = lambda w: pl.BlockSpec((tm, w), lambda i: (i, 0))
    prow = lambda w: pl.BlockSpec((tm, w), lambda i: (jnp.minimum(i, pt - 1), 0))
    srow = lambda w: pl.BlockSpec((tm, w), lambda i: (jnp.maximum(i - pt, 0), 0))
    full = lambda a: pl.BlockSpec(a.shape, lambda i: (0, 0))
    g2 = g_ffn.reshape(1, d)
    return pl.pallas_call(
        functools.partial(_out_router_body, prompt_tiles=pt), grid=(n // tm,),
        in_specs=[row(d), prow(MIX_WIDTH), prow(MEM_WIDTH), srow(MIX_WIDTH), srow(MEM_WIDTH),
                  full(wo_bf16), full(g2), full(wr_hi), full(wr_lo), full(b_route)],
        out_specs=[row(d), row(d), row(LANES), pl.BlockSpec((1, LANES), lambda i: (0, 0))],
        out_shape=[jax.ShapeDtypeStruct((n, d), F32), jax.ShapeDtypeStruct((n, d), F32),
                   jax.ShapeDtypeStruct((n, LANES), F32), jax.ShapeDtypeStruct((1, LANES), F32)],
        scratch_shapes=[pltpu.VMEM((1, LANES), F32)],
        compiler_params=_cparams("arbitrary"), name="out_router",
    )(x, o_p, om_p, o_s, om_s, wo_bf16, g2, wr_hi, wr_lo, b_route)


def _expert_body(be_ref, nused_ref, *refs, piece_starts):
    x_refs = refs[:len(piece_starts)]
    wgu_ref, wdn_ref, y_ref, wgu_s, wdn_s = refs[len(piece_starts):]
    i = pl.program_id(0)
    prev = be_ref[jnp.maximum(i - 1, 0)]

    @pl.when((i == 0) | (be_ref[i] != prev))
    def _():
        wgu_s[...] = wgu_ref[...].astype(BF16)
        wdn_s[...] = wdn_ref[...].astype(BF16)

    @pl.when(i < nused_ref[0])
    def _():
        x = x_refs[0][...]
        for start, ref in zip(piece_starts[1:], x_refs[1:]):
            x = jnp.where(i >= start, ref[...], x)
        h = _dot(x.astype(BF16), wgu_s[...])
        a, u = h[:, :EXPERT_FF], h[:, EXPERT_FF:]
        act = a * (1.0 / (1.0 + jnp.exp(-a))) * u
        y_ref[...] = _dot(act.astype(BF16), wdn_s[...]).astype(BF16)

    @pl.when(i >= nused_ref[0])
    def _():
        y_ref[...] = jnp.zeros_like(y_ref)


def _experts(layer, x_pieces, block_expert, n_used, w_gu, w_dn):
    d = x_pieces[0].shape[1]
    piece_blocks = [p.shape[0] // MOE_ROWS for p in x_pieces]
    piece_starts = [sum(piece_blocks[:j]) for j in range(len(x_pieces))]
    nb = sum(piece_blocks)

    def piece_spec(start, blocks):
        return pl.BlockSpec((MOE_ROWS, d), lambda i, be, nu: (jnp.clip(i - start, 0, blocks - 1), 0))

    return pl.pallas_call(
        functools.partial(_expert_body, piece_starts=tuple(piece_starts)),
        grid_spec=pltpu.PrefetchScalarGridSpec(
            num_scalar_prefetch=2, grid=(nb,),
            in_specs=[piece_spec(s, b) for s, b in zip(piece_starts, piece_blocks)] +
                     [pl.BlockSpec((None, None, d, 2 * EXPERT_FF), lambda i, be, nu: (layer, be[i], 0, 0)),
                      pl.BlockSpec((None, None, EXPERT_FF, d), lambda i, be, nu: (layer, be[i], 0, 0))],
            out_specs=pl.BlockSpec((MOE_ROWS, d), lambda i, be, nu: (i, 0)),
            scratch_shapes=[pltpu.VMEM((d, 2 * EXPERT_FF), BF16), pltpu.VMEM((EXPERT_FF, d), BF16)]),
        out_shape=jax.ShapeDtypeStruct((nb * MOE_ROWS, d), BF16),
        compiler_params=_cparams("arbitrary"), name="experts",
    )(block_expert, n_used, *x_pieces, w_gu, w_dn)


def _moe(layer, y, xn, info, counts, w_gu, w_dn):
    n = y.shape[0]
    gate = info[:, 2:4]
    cnt = counts[0, ROUTE_LANE0:ROUTE_LANE0 + N_EXPERTS].astype(jnp.int32)
    padded = (cnt + MOE_ROWS - 1) // MOE_ROWS * MOE_ROWS
    pad_end = jnp.cumsum(padded)
    pad_start = pad_end - padded
    dest = pad_start[info[:, 0:2].astype(jnp.int32)] + info[:, 4:6].astype(jnp.int32)
    nb = -(-(2 * n + N_EXPERTS * (MOE_ROWS - 1)) // MOE_ROWS)
    block_row0 = jnp.arange(nb, dtype=jnp.int32) * MOE_ROWS
    block_expert = jnp.minimum(jnp.sum((pad_end[None, :] <= block_row0[:, None]).astype(jnp.int32), axis=1),
                               N_EXPERTS - 1)
    n_used = (pad_end[-1:] // MOE_ROWS).astype(jnp.int32)
    tok = jnp.broadcast_to(jnp.arange(n, dtype=jnp.int32)[:, None], (n, 2))
    row_tok = jnp.zeros((nb * MOE_ROWS,), jnp.int32).at[dest.reshape(-1)].set(tok.reshape(-1))
    cuts = [j * nb // GATHER_SPLIT * MOE_ROWS for j in range(GATHER_SPLIT + 1)]
    x_pieces = [xn[row_tok[a:b]] for a, b in zip(cuts[:-1], cuts[1:])]
    ys = _experts(layer, x_pieces, block_expert, n_used, w_gu, w_dn)
    return y + gate[:, 0:1] * ys[dest[:, 0]].astype(F32) + gate[:, 1:2] * ys[dest[:, 1]].astype(F32)


def _final_norm_body(x_ref, g_ref, o_ref):
    o_ref[...] = _rmsnorm_rows(x_ref[...], g_ref[...])


def _final_norm(x, row0, n_rows, g):
    d = x.shape[1]
    tm = ROW_TILE
    t0 = row0 // tm
    return pl.pallas_call(
        _final_norm_body, grid=(n_rows // tm,),
        in_specs=[pl.BlockSpec((tm, d), lambda i: (t0 + i, 0)), pl.BlockSpec((1, d), lambda i: (0, 0))],
        out_specs=pl.BlockSpec((tm, d), lambda i: (i, 0)), out_shape=jax.ShapeDtypeStruct((n_rows, d), F32),
        compiler_params=_cparams("parallel"), name="final_norm",
    )(x, g.reshape(1, d))


def kernel(x_prompt, x_sample, mem_prompt, cache_k, cache_v, cache_mem_k, cache_mem_v, page_table, norm_attn, norm_mem, norm_ffn, norm_final, w_in, w_mem_kv, w_out, diff_lambda, diff_subln, w_router_group, b_router_group, w_router_expert, b_router_expert, w_expert_gu, w_expert_down):
    batch, seq, d = x_prompt.shape
    db, dt, _ = x_sample.shape
    depth = w_in.shape[0]
    n_mem = mem_prompt.shape[1]
    past_len = page_table.shape[1] * cache_k.shape[2]
    n_p, n_s = batch * seq, db * dt
    n = n_p + n_s
    assert seq % ROW_TILE == 0 and n_s % ROW_TILE == 0 and ROW_TILE % dt == 0 and seq % DIFF_TILE == 0

    x = jnp.concatenate([x_prompt.reshape(n_p, d), x_sample.reshape(n_s, d)], axis=0)
    seq_tiles = seq // ROW_TILE
    pos = jnp.concatenate([jnp.arange(seq, dtype=jnp.int32),
                           past_len + jnp.arange(ROW_TILE, dtype=jnp.int32) % dt])
    rot_tables = _rotary_tables(pos)

    mem_k, mem_v = _memkv(mem_prompt.reshape(batch * n_mem, d), norm_mem, w_mem_kv)
    mem_k = mem_k.reshape(depth, batch, n_mem, MEM_WIDTH)
    mem_v = mem_v.reshape(depth, batch, n_mem, MEM_WIDTH)

    kv_p = [jnp.zeros((depth, n_p, MIX_WIDTH), F32)] * 2
    kv_s = [jnp.zeros((depth, n_s, MIX_WIDTH), F32)] * 2
    for i in range(depth):
        is_diff = i % 2 == 1
        diff_params = None
        q_scale = QK_SCALE
        if is_diff:
            lam_init = 0.8 - 0.6 * math.exp(-0.3 * i)
            diff_params = (diff_lambda[i // 2], diff_subln[i // 2], lam_init)
            q_scale = QK_SCALE * LOG2E
        w_in_i = w_in[i].astype(BF16)
        rot_p = (rot_tables, 0, seq_tiles) if is_diff else None
        rot_s = (rot_tables, seq_tiles, 1) if is_diff else None
        *kv_p, qb, kb, vb, qm = _proj(x, 0, n_p, norm_attn[i], w_in_i, q_scale, rot_p, i, kv_p)
        *kv_s, qb_s, kb_s, vb_s, qm_s = _proj(x, n_p, n_s, norm_attn[i], w_in_i, q_scale, rot_s, i, kv_s)

        o = _prompt_attention(qb, kb, vb, batch, seq, diff_params)
        om = _mem_prompt(qm, mem_k[i], mem_v[i], batch, seq)
        s3 = lambda a: a.reshape(db, dt, a.shape[-1])
        o_s, om_s = _sample_attention(i, s3(qb_s), s3(kb_s), s3(vb_s), s3(qm_s), cache_k, cache_v,
                                      cache_mem_k, cache_mem_v, page_table, diff_params)
        o_s = o_s.reshape(n_s, MIX_WIDTH)
        om_s = om_s.reshape(n_s, MEM_WIDTH)

        w_route = jnp.zeros((d, LANES), F32)
        w_route = w_route.at[:, :N_GROUPS].set(w_router_group[i])
        w_route = w_route.at[:, ROUTE_LANE0:ROUTE_LANE0 + N_EXPERTS].set(w_router_expert[i])
        wr_hi = w_route.astype(BF16)
        wr_lo = (w_route - wr_hi.astype(F32)).astype(BF16)
        b_route = jnp.zeros((1, LANES), F32)
        b_route = b_route.at[0, :N_GROUPS].set(b_router_group[i])
        b_route = b_route.at[0, ROUTE_LANE0:ROUTE_LANE0 + N_EXPERTS].set(b_router_expert[i].reshape(-1))
        y, xn, info, counts = _out_router(x, o, om, o_s, om_s, w_out[i].astype(BF16), norm_ffn[i],
                                          wr_hi, wr_lo, b_route)
        x = _moe(i, y, xn, info, counts, w_expert_gu, w_expert_down)

    y_p = _final_norm(x, 0, n_p, norm_final).reshape(batch, seq, d)
    y_s = _final_norm(x, n_p, n_s, norm_final).reshape(db, dt, d)
    prompt = lambda a: a.reshape(depth, batch, seq, MIX_WIDTH)
    sample = lambda a: a.reshape(depth, db, dt, MIX_WIDTH)
    return (y_p, y_s, prompt(kv_p[0]), prompt(kv_p[1]), mem_k, mem_v, sample(kv_s[0]), sample(kv_s[1]))
```

```python
import functools
import math

import jax
import jax.numpy as jnp
from jax import lax
from jax.experimental import pallas as pl
from jax.experimental.pallas import tpu as pltpu

F32 = jnp.float32
BF16 = jnp.bfloat16

HEAD_DIM = 64
SB_HEADS = 12
DIFF_HEADS = 6
MEM_HEADS = 4
MIX_WIDTH = SB_HEADS * HEAD_DIM
MEM_WIDTH = MEM_HEADS * HEAD_DIM
ROT_DIM = HEAD_DIM // 4
ROPE_THETA = 500000.0
N_GROUPS = 8
EXPERTS_PER_GROUP = 8
N_EXPERTS = N_GROUPS * EXPERTS_PER_GROUP
EXPERT_FF = 256
RMS_EPS = 1e-6
QK_SCALE = HEAD_DIM ** -0.5

LANES = 128
ROW_TILE = 512
ATTN_TILE = 256
DIFF_TILE = 512
LOG2E = 1.4426950408889634
MOE_ROWS = 128
GATHER_SPLIT = 3
NEG_BIG = -1e30
SB_DEAD = -104.0
VMEM_LIMIT = 48 * 1024 * 1024
ROUTE_LANE0 = N_GROUPS


def _cparams(*sem):
    return pltpu.CompilerParams(dimension_semantics=sem, vmem_limit_bytes=VMEM_LIMIT)


def _rmsnorm_rows(x, g):
    ms = jnp.mean(x * x, axis=-1, keepdims=True)
    return x * lax.rsqrt(ms + RMS_EPS) * g


def _dot(a, b):
    return jnp.dot(a, b, preferred_element_type=F32)


def _dot_nt(a, b):
    return lax.dot_general(a, b, (((1,), (1,)), ((), ())), preferred_element_type=F32)


def _iota(shape, axis):
    return lax.broadcasted_iota(jnp.int32, shape, axis)


def _proj_body(rotary, q_scale, x_ref, g_ref, w_ref, *rest):
    if rotary:
        cos_ref, sa_ref, sb_ref = rest[:3]
        rest = rest[3:]
    kf_ref, vf_ref, qb_ref, kb_ref, vb_ref, qm_ref = rest[-6:]
    xn = _rmsnorm_rows(x_ref[...], g_ref[...]).astype(BF16)
    q = _dot(xn, w_ref[:, 0:MIX_WIDTH])
    k = _dot(xn, w_ref[:, MIX_WIDTH:2 * MIX_WIDTH])
    v = _dot(xn, w_ref[:, 2 * MIX_WIDTH:3 * MIX_WIDTH])
    qm = _dot(xn, w_ref[:, 3 * MIX_WIDTH:])
    if rotary:
        c, sa, sb = cos_ref[...], sa_ref[...], sb_ref[...]

        def rot(t):
            outs = []
            for j in range(MIX_WIDTH // LANES):
                s = t[:, j * LANES:(j + 1) * LANES]
                outs.append(s * c + pltpu.roll(s, LANES - ROT_DIM // 2, 1) * sa
                            + pltpu.roll(s, ROT_DIM // 2, 1) * sb)
            return jnp.concatenate(outs, axis=1)

        q = rot(q)
        k = rot(k)
    kf_ref[...] = k
    vf_ref[...] = v
    qb_ref[...] = (q * q_scale).astype(BF16)
    kb_ref[...] = k.astype(BF16)
    vb_ref[...] = v.astype(BF16)
    qm_ref[...] = (qm * QK_SCALE).astype(BF16)


def _proj(x, row0, n_rows, g, w_bf16, q_scale, rot, layer, kv_prev):
    d = x.shape[1]
    tm = ROW_TILE
    t0 = row0 // tm
    in_specs = [pl.BlockSpec((tm, d), lambda i: (t0 + i, 0)), pl.BlockSpec((1, d), lambda i: (0, 0)),
                pl.BlockSpec(w_bf16.shape, lambda i: (0, 0))]
    args = [x, g.reshape(1, d), w_bf16]
    if rot is not None:
        tables, first, nblk = rot
        in_specs += [pl.BlockSpec((tm, LANES), lambda i: (first + i % nblk, 0))] * 3
        args += list(tables)
    aliases = {len(args): 0, len(args) + 1: 1}
    in_specs += [pl.BlockSpec(memory_space=pl.ANY)] * 2
    args += list(kv_prev)
    row = lambda w: pl.BlockSpec((tm, w), lambda i: (i, 0))
    kv_spec = pl.BlockSpec((None, tm, MIX_WIDTH), lambda i: (layer, i, 0))
    out_shape = [jax.ShapeDtypeStruct(kv_prev[0].shape, F32)] * 2 + \
                [jax.ShapeDtypeStruct((n_rows, MIX_WIDTH), BF16)] * 3 + \
                [jax.ShapeDtypeStruct((n_rows, MEM_WIDTH), BF16)]
    out_specs = [kv_spec] * 2 + [row(MIX_WIDTH)] * 3 + [row(MEM_WIDTH)]
    return pl.pallas_call(
        functools.partial(_proj_body, rot is not None, q_scale),
        grid=(n_rows // tm,), in_specs=in_specs, out_specs=out_specs, out_shape=out_shape,
        input_output_aliases=aliases,
        compiler_params=_cparams("parallel"), name="proj_rot" if rot is not None else "proj",
    )(*args)


def _rotary_tables(pos):
    half = ROT_DIM // 2
    lane = jnp.arange(LANES) % HEAD_DIM
    inv_freq = 1.0 / (ROPE_THETA ** ((lane % half).astype(F32) * 2.0 / ROT_DIM))
    ang = pos.astype(F32)[:, None] * inv_freq[None, :]
    cos, sin = jnp.cos(ang), jnp.sin(ang)
    c = jnp.where(lane[None, :] < ROT_DIM, cos, 1.0)
    sa = jnp.where(lane[None, :] < half, -sin, 0.0)
    sb = jnp.where((lane[None, :] >= half) & (lane[None, :] < ROT_DIM), sin, 0.0)
    return c, sa, sb


def _memkv_body(mem_ref, g_ref, w_ref, k_ref, v_ref):
    xn = _rmsnorm_rows(mem_ref[...], g_ref[...]).astype(BF16)
    kv = _dot(xn, w_ref[...].astype(BF16))
    k_ref[...] = kv[:, :MEM_WIDTH]
    v_ref[...] = kv[:, MEM_WIDTH:]


def _memkv(mem2d, norm_mem, w_mem_kv):
    depth, d = norm_mem.shape
    m = mem2d.shape[0]
    out = jax.ShapeDtypeStruct((depth, m, MEM_WIDTH), F32)
    ospec = pl.BlockSpec((None, m, MEM_WIDTH), lambda l: (l, 0, 0))
    return pl.pallas_call(
        _memkv_body, grid=(depth,),
        in_specs=[pl.BlockSpec((m, d), lambda l: (0, 0)),
                  pl.BlockSpec((None, 1, d), lambda l: (l, 0, 0)),
                  pl.BlockSpec((None, d, 2 * MEM_WIDTH), lambda l: (l, 0, 0))],
        out_specs=[ospec, ospec], out_shape=[out, out],
        compiler_params=_cparams("parallel"), name="mem_kv",
    )(mem2d, norm_mem.reshape(depth, 1, d), w_mem_kv)


def _neg_upper(tk):
    return jnp.where(_iota((tk, tk), 0) > _iota((tk, tk), 1), -1.0, 0.0).astype(BF16)


def _sb_chunk(qh, ks, vs, uneg, carry, acc, mask):
    z = _dot_nt(qh, ks)
    sp = jnp.maximum(z, 0.0) + jnp.log(1.0 + jnp.exp(-jnp.abs(z)))
    spm = sp if mask is None else jnp.where(mask, sp, 0.0)
    hi = spm.astype(BF16)
    lo = (spm - hi.astype(F32)).astype(BF16)
    between = _dot(hi, uneg) + _dot(lo, uneg) + carry
    a = jnp.exp(z - sp + between)
    if mask is not None:
        a = jnp.where(mask, a, 0.0)
    acc = acc + _dot(a.astype(BF16), vs)
    carry = carry - jnp.sum(spm, axis=-1, keepdims=True)
    return carry, acc


def _sm_chunk(qh, ks, vs, m, l, acc, mask):
    s = _dot_nt(qh, ks)
    if mask is not None:
        s = jnp.where(mask, s, NEG_BIG)
    m_new = jnp.maximum(m, jnp.max(s, axis=-1, keepdims=True))
    alpha = jnp.exp2(m - m_new)
    p = jnp.exp2(s - jnp.concatenate([m_new] * (s.shape[1] // LANES), axis=1))
    l = alpha * l + jnp.sum(p, axis=-1, keepdims=True)
    acc = alpha * acc + _dot(p.astype(BF16), vs)
    return m_new, l, acc


def _diff_lambda(lam_ref, lam_init):
    lv = lam_ref[...]
    s1 = jnp.sum(lv[0:1] * lv[1:2], axis=-1, keepdims=True)
    s2 = jnp.sum(lv[2:3] * lv[3:4], axis=-1, keepdims=True)
    return jnp.exp(s1) - jnp.exp(s2) + lam_init


def _head_rmsnorm(o, gain, lam_init):
    return _rmsnorm_rows(o, gain) * (1.0 - lam_init)


def _mem_attention(qbd, mk, mv):
    s = _dot_nt(qbd, mk)
    p = jnp.exp(s - jnp.max(s, axis=-1, keepdims=True))
    o = _dot(p.astype(BF16), mv)
    return o / jnp.sum(p, axis=-1, keepdims=True)


def _stack_lane_halves(q, lane):
    return jnp.concatenate([jnp.where(lane < HEAD_DIM, q, 0.0), jnp.where(lane >= HEAD_DIM, q, 0.0)],
                           axis=0).astype(BF16)


def _sb_prompt_body(q_ref, k_ref, v_ref, o_ref, acc_ref, carry_ref, *, t):
    i = pl.program_id(2)
    lane = _iota((t, LANES), 1)
    qq = _stack_lane_halves(q_ref[...].astype(F32), lane)
    uneg = _neg_upper(t)
    causal = _iota((2 * t, t), 1) < (_iota((2 * t, t), 0) & (t - 1))

    def process(c, mask):
        start = pl.multiple_of(c * t, t)
        carry, acc = _sb_chunk(qq, k_ref[pl.ds(start, t), :], v_ref[pl.ds(start, t), :], uneg,
                               carry_ref[...], acc_ref[...], mask)
        carry_ref[...] = carry
        acc_ref[...] = acc

    def alive():
        return (jnp.max(carry_ref[...]) > SB_DEAD).astype(jnp.int32)

    carry_ref[...] = jnp.zeros_like(carry_ref)
    acc_ref[...] = jnp.zeros_like(acc_ref)

    @pl.when(i == 0)
    def _():
        process(i, causal)

    @pl.when(i > 0)
    def _():
        process(i, causal)
        process(i - 1, None)

    def body(st):
        process(st[0], None)
        return st[0] - 1, alive()

    lax.while_loop(lambda st: (st[0] >= 0) & (st[1] > 0), body, (i - 2, alive()))
    o_ref[...] = jnp.where(lane < HEAD_DIM, acc_ref[0:t], acc_ref[t:2 * t]).astype(BF16)


def _diff_prompt_body(lam_ref, gain_ref, q_ref, k_ref, v_ref, o_ref, m_ref, l_ref, acc_ref, *, t, lam_init):
    i = pl.program_id(2)
    qq = _stack_lane_halves(q_ref[...].astype(F32), _iota((t, LANES), 1))

    def process(c, mask):
        start = pl.multiple_of(c * t, t)
        m, l, acc = _sm_chunk(qq, k_ref[pl.ds(start, t), :], v_ref[pl.ds(start, t), :],
                              m_ref[...], l_ref[...], acc_ref[...], mask)
        m_ref[...] = m
        l_ref[...] = l
        acc_ref[...] = acc

    m_ref[...] = jnp.full_like(m_ref, NEG_BIG)
    l_ref[...] = jnp.zeros_like(l_ref)
    acc_ref[...] = jnp.zeros_like(acc_ref)

    def body(c, carry):
        process(c, None)
        return carry

    lax.fori_loop(0, i, body, 0)
    process(i, _iota((2 * t, t), 1) <= (_iota((2 * t, t), 0) & (t - 1)))
    lam = _diff_lambda(lam_ref, lam_init)
    o = acc_ref[0:t] / l_ref[0:t] - lam * (acc_ref[t:2 * t] / l_ref[t:2 * t])
    o_ref[...] = _head_rmsnorm(o, gain_ref[...], lam_init).astype(BF16)


def _prompt_attention(qb, kb, vb, batch, seq, diff_params):
    n_rows = batch * seq
    t = min(ATTN_TILE if diff_params is None else DIFF_TILE, seq)
    nq = seq // t
    nh = MIX_WIDTH // LANES
    qspec = pl.BlockSpec((t, LANES), lambda b, h, i: (b * nq + i, h))
    kvspec = pl.BlockSpec((seq, LANES), lambda b, h, i: (b, h))
    out_shape = jax.ShapeDtypeStruct((n_rows, MIX_WIDTH), BF16)
    if diff_params is None:
        body = functools.partial(_sb_prompt_body, t=t)
        in_specs, args = [qspec, kvspec, kvspec], [qb, kb, vb]
        scratch = [pltpu.VMEM((2 * t, LANES), F32), pltpu.VMEM((2 * t, 1), F32)]
        name = "sb_prompt"
    else:
        lam_vecs, gain, lam_init = diff_params
        body = functools.partial(_diff_prompt_body, t=t, lam_init=lam_init)
        in_specs = [pl.BlockSpec((4, HEAD_DIM), lambda b, h, i: (0, 0)),
                    pl.BlockSpec((1, LANES), lambda b, h, i: (0, 0)), qspec, kvspec, kvspec]
        args = [lam_vecs, gain.reshape(1, LANES), qb, kb, vb]
        scratch = [pltpu.VMEM((2 * t, LANES), F32)] * 3
        name = "diff_prompt"
    return pl.pallas_call(
        body, grid=(batch, nh, nq), in_specs=in_specs, out_specs=qspec, out_shape=out_shape,
        scratch_shapes=scratch, compiler_params=_cparams("parallel", "parallel", "arbitrary"), name=name,
    )(*args)


def _mem_prompt_body(q_ref, mk_ref, mv_ref, o_ref):
    q = q_ref[...].astype(F32)
    mk = mk_ref[...].astype(BF16)
    mv = mv_ref[...].astype(BF16)
    lane_head = _iota(q.shape, 1) // HEAD_DIM
    out = jnp.zeros(q.shape, F32)
    for h in range(MEM_HEADS):
        o = _mem_attention(jnp.where(lane_head == h, q, 0.0).astype(BF16), mk, mv)
        out = jnp.where(lane_head == h, o, out)
    o_ref[...] = out.astype(BF16)


def _mem_prompt(qm, mk, mv, batch, seq):
    n_rows = batch * seq
    t = min(ROW_TILE, seq)
    nq = seq // t
    n_mem = mk.shape[1]
    qspec = pl.BlockSpec((t, MEM_WIDTH), lambda b, i: (b * nq + i, 0))
    mspec = pl.BlockSpec((None, n_mem, MEM_WIDTH), lambda b, i: (b, 0, 0))
    return pl.pallas_call(
        _mem_prompt_body, grid=(batch, nq), in_specs=[qspec, mspec, mspec], out_specs=qspec,
        out_shape=jax.ShapeDtypeStruct((n_rows, MEM_WIDTH), BF16),
        compiler_params=_cparams("parallel", "parallel"), name="mem_prompt",
    )(qm, mk, mv)


def _block_diag_queries(q, n_blocks, block_lanes):
    dt = q.shape[0]
    qt = jnp.concatenate([q] * n_blocks, axis=0)
    keep = (_iota(qt.shape, 0) // dt) == (_iota(qt.shape, 1) // block_lanes)
    return jnp.where(keep, qt, 0.0).astype(BF16)


def _fold_blocks(full, dt, block_lanes):
    keep = (_iota(full.shape, 0) // dt) == (_iota(full.shape, 1) // block_lanes)
    sel = jnp.where(keep, full, 0.0)
    out = sel[0:dt]
    for j in range(1, full.shape[0] // dt):
        out = out + sel[j * dt:(j + 1) * dt]
    return out


def _sample_body(pt_ref, *refs, n_pages, page, dt, diff, lam_init):
    del pt_ref
    if diff:
        lam_ref, gain_ref = refs[:2]
        refs = refs[2:]
    q_ref, kn_ref, vn_ref, qm_ref, mk_ref, mv_ref = refs[:6]
    k_pages = refs[6:6 + n_pages]
    v_pages = refs[6 + n_pages:6 + 2 * n_pages]
    o_ref, om_ref, kall, vall = refs[6 + 2 * n_pages:]

    n_maps = MIX_WIDTH // HEAD_DIM
    rows = n_maps * dt
    past = n_pages * page
    qbd = _block_diag_queries(q_ref[...].astype(F32), n_maps, HEAD_DIM)
    for p in range(n_pages):
        kall[p * page:(p + 1) * page, :] = k_pages[p][...].astype(BF16)
        vall[p * page:(p + 1) * page, :] = v_pages[p][...].astype(BF16)
    pad = jnp.zeros((page - dt, MIX_WIDTH), F32)
    kall[past:, :] = jnp.concatenate([kn_ref[...].astype(F32), pad], axis=0).astype(BF16)
    vall[past:, :] = jnp.concatenate([vn_ref[...].astype(F32), pad], axis=0).astype(BF16)
    tpos = _iota((rows, page), 0) % dt
    col = _iota((rows, page), 1)
    z = _dot_nt(qbd, kall[...])
    chunks = [z[:, c * page:(c + 1) * page] for c in range(n_pages + 1)]

    if not diff:
        new_ok = col < tpos
        sp = [jnp.maximum(zc, 0.0) + jnp.log(1.0 + jnp.exp(-jnp.abs(zc))) for zc in chunks]
        spm = sp[:-1] + [jnp.where(new_ok, sp[-1], 0.0)]
        stacked = jnp.concatenate(spm, axis=0)
        hi = stacked.astype(BF16)
        lo = (stacked - hi.astype(F32)).astype(BF16)
        uneg = _neg_upper(page)
        later = _dot(hi, uneg) + _dot(lo, uneg)
        carry = jnp.zeros((rows, 1), F32)
        a = [None] * (n_pages + 1)
        for c in reversed(range(n_pages + 1)):
            a[c] = jnp.exp(chunks[c] - sp[c] + later[c * rows:(c + 1) * rows] + carry)
            carry = carry - jnp.sum(spm[c], axis=-1, keepdims=True)
        a[-1] = jnp.where(new_ok, a[-1], 0.0)
        acc = _dot(jnp.concatenate(a, axis=1).astype(BF16), vall[...])
        o = _fold_blocks(acc, dt, HEAD_DIM)
    else:
        s = jnp.concatenate(chunks[:-1] + [jnp.where(col <= tpos, chunks[-1], NEG_BIG)], axis=1)
        p = jnp.exp2(s - jnp.max(s, axis=-1, keepdims=True))
        l = jnp.sum(p, axis=-1, keepdims=True)
        acc = _dot(p.astype(BF16), vall[...])
        lam = _diff_lambda(lam_ref, lam_init)
        second_map = ((_iota((rows, 1), 0) // dt) % 2) == 1
        weight = jnp.where(second_map, -lam, 1.0) / l
        o = _fold_blocks(acc * weight, 2 * dt, 2 * HEAD_DIM)
        o = o[0:dt] + o[dt:2 * dt]
        gain = gain_ref[...]
        o = jnp.concatenate(
            [_head_rmsnorm(o[:, j * LANES:(j + 1) * LANES], gain, lam_init) for j in range(DIFF_HEADS)], axis=1)
    o_ref[...] = o.astype(BF16)

    qmbd = _block_diag_queries(qm_ref[...].astype(F32), MEM_HEADS, HEAD_DIM)
    om = _mem_attention(qmbd, mk_ref[...].astype(BF16), mv_ref[...].astype(BF16))
    om_ref[...] = _fold_blocks(om, dt, HEAD_DIM).astype(BF16)


def _sample_attention(layer, q3, kn3, vn3, qm3, cache_k, cache_v, cache_mem_k, cache_mem_v, page_table,
                      diff_params):
    db, dt, _ = q3.shape
    n_pages = page_table.shape[1]
    page = cache_k.shape[2]
    n_mem = cache_mem_k.shape[2]
    diff = diff_params is not None

    def req(w):
        return pl.BlockSpec((None, dt, w), lambda b, pt: (b, 0, 0))

    def page_spec(p):
        return pl.BlockSpec((None, None, page, MIX_WIDTH), lambda b, pt: (layer, pt[b * n_pages + p], 0, 0))

    mem_spec = pl.BlockSpec((None, None, n_mem, MEM_WIDTH), lambda b, pt: (layer, b, 0, 0))
    in_specs = [req(MIX_WIDTH)] * 3 + [req(MEM_WIDTH), mem_spec, mem_spec]
    in_specs += [page_spec(p) for p in range(n_pages)] * 2
    args = [q3, kn3, vn3, qm3, cache_mem_k, cache_mem_v] + [cache_k] * n_pages + [cache_v] * n_pages
    lam_init = 0.0
    if diff:
        lam_vecs, gain, lam_init = diff_params
        in_specs = [pl.BlockSpec((4, HEAD_DIM), lambda b, pt: (0, 0)),
                    pl.BlockSpec((1, LANES), lambda b, pt: (0, 0))] + in_specs
        args = [lam_vecs, gain.reshape(1, LANES)] + args
    body = functools.partial(_sample_body, n_pages=n_pages, page=page, dt=dt, diff=diff, lam_init=lam_init)
    return pl.pallas_call(
        body,
        grid_spec=pltpu.PrefetchScalarGridSpec(
            num_scalar_prefetch=1, grid=(db,), in_specs=in_specs,
            out_specs=[req(MIX_WIDTH), req(MEM_WIDTH)],
            scratch_shapes=[pltpu.VMEM(((n_pages + 1) * page, MIX_WIDTH), BF16)] * 2),
        out_shape=[jax.ShapeDtypeStruct((db, dt, MIX_WIDTH), BF16),
                   jax.ShapeDtypeStruct((db, dt, MEM_WIDTH), BF16)],
        compiler_params=_cparams("arbitrary"), name="diff_sample" if diff else "sb_sample",
    )(page_table.reshape(-1), *args)


def _out_router_body(x_ref, op_ref, omp_ref, os_ref, oms_ref, wo_ref, g_ref, wrh_ref, wrl_ref, br_ref,
                     y_ref, xn_ref, info_ref, cnt_ref, run_ref, *, prompt_tiles):
    @pl.when(pl.program_id(0) == 0)
    def _():
        run_ref[...] = jnp.zeros_like(run_ref)

    is_prompt = pl.program_id(0) < prompt_tiles
    o = jnp.where(is_prompt, op_ref[...], os_ref[...])
    om = jnp.where(is_prompt, omp_ref[...], oms_ref[...])
    y = x_ref[...] + _dot(o, wo_ref[0:MIX_WIDTH, :]) + _dot(om, wo_ref[MIX_WIDTH:, :])
    y_ref[...] = y
    xn = _rmsnorm_rows(y, g_ref[...])
    xn_ref[...] = xn
    xh = xn.astype(BF16)
    xl = (xn - xh.astype(F32)).astype(BF16)
    wh, wl = wrh_ref[...], wrl_ref[...]
    logit = _dot(xh, wh) + _dot(xl, wh) + _dot(xh, wl) + br_ref[...]

    tm = logit.shape[0]
    lane = _iota((tm, LANES), 1)
    lg = jnp.where(lane < N_GROUPS, logit, -jnp.inf)
    mg = jnp.max(lg, axis=-1, keepdims=True)
    g_idx = jnp.min(jnp.where(lg == mg, lane, LANES), axis=-1, keepdims=True)
    g_w = 1.0 / jnp.sum(jnp.exp(lg - mg), axis=-1, keepdims=True)
    in_group = (lane >= ROUTE_LANE0) & (((lane - ROUTE_LANE0) >> 3) == g_idx) & (lane < ROUTE_LANE0 + N_EXPERTS)
    le = jnp.where(in_group, logit, -jnp.inf)
    v1 = jnp.max(le, axis=-1, keepdims=True)
    i1 = jnp.min(jnp.where(le == v1, lane, LANES), axis=-1, keepdims=True)
    le2 = jnp.where(lane == i1, -jnp.inf, le)
    v2 = jnp.max(le2, axis=-1, keepdims=True)
    i2 = jnp.min(jnp.where(le2 == v2, lane, LANES), axis=-1, keepdims=True)
    e21 = jnp.exp(v2 - v1)
    w1 = 1.0 / (1.0 + e21)
    w2 = e21 * w1

    hit1, hit2 = lane == i1, lane == i2
    onehot = jnp.where(hit1 | hit2, 1.0, 0.0)
    lower = jnp.where(_iota((tm, tm), 1) < _iota((tm, tm), 0), 1.0, 0.0).astype(BF16)
    before = _dot(lower, onehot.astype(BF16)) + run_ref[...]
    r1 = jnp.sum(jnp.where(hit1, before, 0.0), axis=-1, keepdims=True)
    r2 = jnp.sum(jnp.where(hit2, before, 0.0), axis=-1, keepdims=True)
    run_ref[...] = run_ref[...] + jnp.sum(onehot, axis=0, keepdims=True)
    cnt_ref[...] = run_ref[...]

    info = jnp.zeros((tm, LANES), F32)
    fields = ((i1 - ROUTE_LANE0).astype(F32), (i2 - ROUTE_LANE0).astype(F32), g_w * w1, g_w * w2, r1, r2)
    for j, val in enumerate(fields):
        info = jnp.where(lane == j, val, info)
    info_ref[...] = info


def _out_router(x, o_p, om_p, o_s, om_s, wo_bf16, g_ffn, wr_hi, wr_lo, b_route):
    n, d = x.shape
    tm = ROW_TILE
    pt = o_p.shape[0] // tm
    row = lambda w: pl.BlockSpec((tm, w), lambda i: (i, 0))
    prow = lambda w: pl.BlockSpec((tm, w), lambda i: (jnp.minimum(i, pt - 1), 0))
    srow = lambda w: pl.BlockSpec((tm, w), lambda i: (jnp.maximum(i - pt, 0), 0))
    full = lambda a: pl.BlockSpec(a.shape, lambda i: (0, 0))
    g2 = g_ffn.reshape(1, d)
    return pl.pallas_call(
        functools.partial(_out_router_body, prompt_tiles=pt), grid=(n // tm,),
        in_specs=[row(d), prow(MIX_WIDTH), prow(MEM_WIDTH), srow(MIX_WIDTH), srow(MEM_WIDTH),
                  full(wo_bf16), full(g2), full(wr_hi), full(wr_lo), full(b_route)],
        out_specs=[row(d), row(d), row(LANES), pl.BlockSpec((1, LANES), lambda i: (0, 0))],
        out_shape=[jax.ShapeDtypeStruct((n, d), F32), jax.ShapeDtypeStruct((n, d), F32),
                   jax.ShapeDtypeStruct((n, LANES), F32), jax.ShapeDtypeStruct((1, LANES), F32)],
        scratch_shapes=[pltpu.VMEM((1, LANES), F32)],
        compiler_params=_cparams("arbitrary"), name="out_router",
    )(x, o_p, om_p, o_s, om_s, wo_bf16, g2, wr_hi, wr_lo, b_route)


def _expert_body(be_ref, nused_ref, x_ref, wgu_ref, wdn_ref, ys_in_ref, y_ref, wgu_s, wdn_s, *, block0):
    del ys_in_ref
    i = pl.program_id(0)
    blk = block0 + i
    prev = be_ref[jnp.maximum(blk - 1, 0)]

    @pl.when((i == 0) | (be_ref[blk] != prev))
    def _():
        wgu_s[...] = wgu_ref[...].astype(BF16)
        wdn_s[...] = wdn_ref[...].astype(BF16)

    @pl.when(blk < nused_ref[0])
    def _():
        h = _dot(x_ref[...].astype(BF16), wgu_s[...])
        a, u = h[:, :EXPERT_FF], h[:, EXPERT_FF:]
        act = a * (1.0 / (1.0 + jnp.exp(-a))) * u
        y_ref[...] = _dot(act.astype(BF16), wdn_s[...]).astype(BF16)

    @pl.when(blk >= nused_ref[0])
    def _():
        y_ref[...] = jnp.zeros_like(y_ref)


def _experts(layer, x_piece, block0, ys, block_expert, n_used, w_gu, w_dn):
    d = x_piece.shape[1]
    blocks = x_piece.shape[0] // MOE_ROWS
    return pl.pallas_call(
        functools.partial(_expert_body, block0=block0),
        grid_spec=pltpu.PrefetchScalarGridSpec(
            num_scalar_prefetch=2, grid=(blocks,),
            in_specs=[pl.BlockSpec((MOE_ROWS, d), lambda i, be, nu: (i, 0)),
                      pl.BlockSpec((None, None, d, 2 * EXPERT_FF),
                                   lambda i, be, nu: (layer, be[block0 + i], 0, 0)),
                      pl.BlockSpec((None, None, EXPERT_FF, d),
                                   lambda i, be, nu: (layer, be[block0 + i], 0, 0)),
                      pl.BlockSpec(memory_space=pl.ANY)],
            out_specs=pl.BlockSpec((MOE_ROWS, d), lambda i, be, nu: (block0 + i, 0)),
            scratch_shapes=[pltpu.VMEM((d, 2 * EXPERT_FF), BF16), pltpu.VMEM((EXPERT_FF, d), BF16)]),
        out_shape=jax.ShapeDtypeStruct(ys.shape, BF16),
        input_output_aliases={5: 0},
        compiler_params=_cparams("arbitrary"), name="experts",
    )(block_expert, n_used, x_piece, w_gu, w_dn, ys)


def _moe(layer, y, xn, info, counts, w_gu, w_dn):
    n = y.shape[0]
    gate = info[:, 2:4]
    cnt = counts[0, ROUTE_LANE0:ROUTE_LANE0 + N_EXPERTS].astype(jnp.int32)
    padded = (cnt + MOE_ROWS - 1) // MOE_ROWS * MOE_ROWS
    pad_end = jnp.cumsum(padded)
    pad_start = pad_end - padded
    dest = pad_start[info[:, 0:2].astype(jnp.int32)] + info[:, 4:6].astype(jnp.int32)
    nb = -(-(2 * n + N_EXPERTS * (MOE_ROWS - 1)) // MOE_ROWS)
    block_row0 = jnp.arange(nb, dtype=jnp.int32) * MOE_ROWS
    block_expert = jnp.minimum(jnp.sum((pad_end[None, :] <= block_row0[:, None]).astype(jnp.int32), axis=1),
                               N_EXPERTS - 1)
    n_used = (pad_end[-1:] // MOE_ROWS).astype(jnp.int32)
    tok = jnp.broadcast_to(jnp.arange(n, dtype=jnp.int32)[:, None], (n, 2))
    row_tok = jnp.zeros((nb * MOE_ROWS,), jnp.int32).at[dest.reshape(-1)].set(tok.reshape(-1))
    cuts = [j * nb // GATHER_SPLIT * MOE_ROWS for j in range(GATHER_SPLIT + 1)]
    ys = jnp.zeros((nb * MOE_ROWS, xn.shape[1]), BF16)
    for a, b in zip(cuts[:-1], cuts[1:]):
        ys = _experts(layer, xn[row_tok[a:b]], a // MOE_ROWS, ys, block_expert, n_used, w_gu, w_dn)
    return y + gate[:, 0:1] * ys[dest[:, 0]].astype(F32) + gate[:, 1:2] * ys[dest[:, 1]].astype(F32)


def _final_norm_body(x_ref, g_ref, o_ref):
    o_ref[...] = _rmsnorm_rows(x_ref[...], g_ref[...])


def _final_norm(x, row0, n_rows, g):
    d = x.shape[1]
    tm = ROW_TILE
    t0 = row0 // tm
    return pl.pallas_call(
        _final_norm_body, grid=(n_rows // tm,),
        in_specs=[pl.BlockSpec((tm, d), lambda i: (t0 + i, 0)), pl.BlockSpec((1, d), lambda i: (0, 0))],
        out_specs=pl.BlockSpec((tm, d), lambda i: (i, 0)), out_shape=jax.ShapeDtypeStruct((n_rows, d), F32),
        compiler_params=_cparams("parallel"), name="final_norm",
    )(x, g.reshape(1, d))


def kernel(x_prompt, x_sample, mem_prompt, cache_k, cache_v, cache_mem_k, cache_mem_v, page_table, norm_attn, norm_mem, norm_ffn, norm_final, w_in, w_mem_kv, w_out, diff_lambda, diff_subln, w_router_group, b_router_group, w_router_expert, b_router_expert, w_expert_gu, w_expert_down):
    batch, seq, d = x_prompt.shape
    db, dt, _ = x_sample.shape
    depth = w_in.shape[0]
    n_mem = mem_prompt.shape[1]
    past_len = page_table.shape[1] * cache_k.shape[2]
    n_p, n_s = batch * seq, db * dt
    n = n_p + n_s
    assert seq % ROW_TILE == 0 and n_s % ROW_TILE == 0 and ROW_TILE % dt == 0 and seq % DIFF_TILE == 0

    x = jnp.concatenate([x_prompt.reshape(n_p, d), x_sample.reshape(n_s, d)], axis=0)
    seq_tiles = seq // ROW_TILE
    pos = jnp.concatenate([jnp.arange(seq, dtype=jnp.int32),
                           past_len + jnp.arange(ROW_TILE, dtype=jnp.int32) % dt])
    rot_tables = _rotary_tables(pos)

    mem_k, mem_v = _memkv(mem_prompt.reshape(batch * n_mem, d), norm_mem, w_mem_kv)
    mem_k = mem_k.reshape(depth, batch, n_mem, MEM_WIDTH)
    mem_v = mem_v.reshape(depth, batch, n_mem, MEM_WIDTH)

    kv_p = [jnp.zeros((depth, n_p, MIX_WIDTH), F32)] * 2
    kv_s = [jnp.zeros((depth, n_s, MIX_WIDTH), F32)] * 2
    for i in range(depth):
        is_diff = i % 2 == 1
        diff_params = None
        q_scale = QK_SCALE
        if is_diff:
            lam_init = 0.8 - 0.6 * math.exp(-0.3 * i)
            diff_params = (diff_lambda[i // 2], diff_subln[i // 2], lam_init)
            q_scale = QK_SCALE * LOG2E
        w_in_i = w_in[i].astype(BF16)
        rot_p = (rot_tables, 0, seq_tiles) if is_diff else None
        rot_s = (rot_tables, seq_tiles, 1) if is_diff else None
        *kv_p, qb, kb, vb, qm = _proj(x, 0, n_p, norm_attn[i], w_in_i, q_scale, rot_p, i, kv_p)
        *kv_s, qb_s, kb_s, vb_s, qm_s = _proj(x, n_p, n_s, norm_attn[i], w_in_i, q_scale, rot_s, i, kv_s)

        o = _prompt_attention(qb, kb, vb, batch, seq, diff_params)
        om = _mem_prompt(qm, mem_k[i], mem_v[i], batch, seq)
        s3 = lambda a: a.reshape(db, dt, a.shape[-1])
        o_s, om_s = _sample_attention(i, s3(qb_s), s3(kb_s), s3(vb_s), s3(qm_s), cache_k, cache_v,
                                      cache_mem_k, cache_mem_v, page_table, diff_params)
        o_s = o_s.reshape(n_s, MIX_WIDTH)
        om_s = om_s.reshape(n_s, MEM_WIDTH)

        w_route = jnp.zeros((d, LANES), F32)
        w_route = w_route.at[:, :N_GROUPS].set(w_router_group[i])
        w_route = w_route.at[:, ROUTE_LANE0:ROUTE_LANE0 + N_EXPERTS].set(w_router_expert[i])
        wr_hi = w_route.astype(BF16)
        wr_lo = (w_route - wr_hi.astype(F32)).astype(BF16)
        b_route = jnp.zeros((1, LANES), F32)
        b_route = b_route.at[0, :N_GROUPS].set(b_router_group[i])
        b_route = b_route.at[0, ROUTE_LANE0:ROUTE_LANE0 + N_EXPERTS].set(b_router_expert[i].reshape(-1))
        y, xn, info, counts = _out_router(x, o, om, o_s, om_s, w_out[i].astype(BF16), norm_ffn[i],
                                          wr_hi, wr_lo, b_route)
        x = _moe(i, y, xn, info, counts, w_expert_gu, w_expert_down)

    y_p = _final_norm(x, 0, n_p, norm_final).reshape(batch, seq, d)
    y_s = _final_norm(x, n_p, n_s, norm_final).reshape(db, dt, d)
    prompt = lambda a: a.reshape(depth, batch, seq, MIX_WIDTH)
    sample = lambda a: a.reshape(depth, db, dt, MIX_WIDTH)
    return (y_p, y_s, prompt(kv_p[0]), prompt(kv_p[1]), mem_k, mem_v, sample(kv_s[0]), sample(kv_s[1]))
```
